```python
import functools
import jax, jax.numpy as jnp
from jax import lax
import numpy as np

D_MODEL = 1024
BATCH = 2
SEQ = 8192
DEPTH = 4
DEC_BATCH = 128
DEC_SEQ = 8
PAST_LEN = 2048
PAGE_SIZE = 128

GLA_HEADS = 4
GLA_DK = D_MODEL // 2
GLA_DV = D_MODEL
GLA_DK_HEAD = GLA_DK // GLA_HEADS
GLA_DV_HEAD = GLA_DV // GLA_HEADS
GLA_RANK = 16
GLA_GATE_NORM = 16.0
GLA_CHUNK = 64
DIL_PAIRS = ((128, 1), (512, 4), (2048, 16))
N_GROUPS = 3
GROUP_HEADS = 4
HEAD_DIM = 128
B_HEADS = N_GROUPS * GROUP_HEADS
B_WIDTH = B_HEADS * HEAD_DIM
B_OUT = GROUP_HEADS * HEAD_DIM
Q_BLOCK = 128
REL_BUCKETS = 32
REL_MAX_EXACT = 16
REL_MAX_DIST = 2048
D_FF = 4 * D_MODEL
N_MOD = 6
EPS = 1e-6

kernel_name = 'griffin_gla_dilated_hybrid_step'


def in_sizes():
    return (GLA_DK, GLA_DK, GLA_DV, GLA_DV, GLA_RANK, B_WIDTH, B_WIDTH, B_WIDTH, D_MODEL, D_MODEL)


def in_splits():
    return [int(v) for v in np.cumsum(in_sizes())[:-1]]


def rmsnorm(x, g):
    xf = x.astype(jnp.float32)
    return xf * lax.rsqrt(jnp.mean(xf * xf, axis=-1, keepdims=True) + EPS) * g.astype(jnp.float32)


def modulate(u, shift, scale):
    return u * (1.0 + scale[:, None, :]) + shift[:, None, :]


def t5_bucket(dist):
    dist = np.asarray(dist)
    large = REL_MAX_EXACT + (np.log(np.maximum(dist, 1) / REL_MAX_EXACT)
                             / np.log(REL_MAX_DIST / REL_MAX_EXACT)
                             * (REL_BUCKETS - REL_MAX_EXACT)).astype(np.int32)
    large = np.minimum(large, REL_BUCKETS - 1)
    return np.where(dist < REL_MAX_EXACT, dist, large).astype(np.int32)


def group_biases(rel_bias):
    out = []
    for g, (w, d) in enumerate(DIL_PAIRS):
        taps = w // d + 1
        idx = t5_bucket(d * np.arange(taps))
        out.append(rel_bias[idx][:, g * GROUP_HEADS:(g + 1) * GROUP_HEADS].astype(jnp.float32))
    return out


def gla_chunk(S, q, k, v, log_a):
    C = q.shape[2]
    b = jnp.cumsum(log_a, axis=2)
    o_inter = jnp.einsum('bhtd,bhde->bhte', q * jnp.exp(b), S)
    causal = jnp.tril(jnp.ones((C, C), dtype=bool))[None, None, :, :, None]
    diff = b[:, :, :, None, :] - b[:, :, None, :, :]
    decay = jnp.where(causal, jnp.exp(jnp.where(causal, diff, 0.0)), 0.0)
    scores = jnp.einsum('bhtd,bhsd,bhtsd->bhts', q, k, decay)
    o = o_inter + jnp.einsum('bhts,bhse->bhte', scores, v)
    b_last = b[:, :, -1:, :]
    S_new = (jnp.exp(b_last[:, :, 0, :])[..., None] * S
             + jnp.einsum('bhsd,bhse->bhde', k * jnp.exp(b_last - b), v))
    return S_new, o


def gla_prompt(q, k, v, log_a):
    B, T = q.shape[0], q.shape[1]
    n = T // GLA_CHUNK

    def to_chunks(t):
        return t.reshape(B, n, GLA_CHUNK, GLA_HEADS, t.shape[-1]).transpose(1, 0, 3, 2, 4)

    S0 = jnp.zeros((B, GLA_HEADS, GLA_DK_HEAD, GLA_DV_HEAD), jnp.float32)
    S_fin, o = lax.scan(lambda s, inp: gla_chunk(s, *inp), S0,
                        (to_chunks(q), to_chunks(k), to_chunks(v), to_chunks(log_a)))
    o = o.transpose(1, 0, 3, 2, 4).reshape(B, T, GLA_HEADS, GLA_DV_HEAD)
    return o, S_fin


def gla_sample(q, k, v, log_a, S_in):
    tr = lambda t: t.transpose(0, 2, 1, 3)
    S_new, o = gla_chunk(S_in.astype(jnp.float32), tr(q), tr(k), tr(v), tr(log_a))
    return tr(o), S_new


def dilated_group(q_g, k_ext, v_ext, qidx, dil, bias_g):
    taps = bias_g.shape[0]
    idx = qidx[:, None] - dil * jnp.arange(taps)[None, :]
    valid = idx >= 0
    idx_c = jnp.maximum(idx, 0)
    k_sel = k_ext[:, idx_c].astype(jnp.float32)
    v_sel = v_ext[:, idx_c].astype(jnp.float32)
    logits = (jnp.einsum('bqhd,bqihd->bqhi', q_g.astype(jnp.float32), k_sel) * (HEAD_DIM ** -0.5)
              + bias_g.T[None, None])
    logits = jnp.where(valid[None, :, None, :], logits, -1e30)
    m = jnp.max(logits, axis=-1, keepdims=True)
    p = jnp.exp(logits - m)
    s = jnp.sum(p, axis=-1)
    o = jnp.einsum('bqhi,bqihd->bqhd', p, v_sel) / s[..., None]
    lse = m[..., 0] + jnp.log(s)
    return o, lse


def combine_groups(outs):
    o = jnp.stack([t[0] for t in outs], axis=0)
    lse = jnp.stack([t[1] for t in outs], axis=0)
    w = jax.nn.softmax(lse, axis=0)
    return jnp.sum(w[..., None] * o, axis=0)


def gsl(g):
    return slice(g * GROUP_HEADS, (g + 1) * GROUP_HEADS)


def dilated_prompt(q, k, v, biases):
    B, T = q.shape[0], q.shape[1]
    qb = min(Q_BLOCK, T)
    nb = T // qb
    q_blocks = q.reshape(B, nb, qb, B_HEADS, HEAD_DIM).swapaxes(0, 1)

    def block(args):
        q_blk, start = args
        qidx = start + jnp.arange(qb)
        outs = [dilated_group(q_blk[:, :, gsl(g)], k[:, :, gsl(g)], v[:, :, gsl(g)], qidx, d, biases[g])
                for g, (w, d) in enumerate(DIL_PAIRS)]
        return combine_groups(outs)

    o = lax.map(block, (q_blocks, jnp.arange(nb) * qb))
    return o.swapaxes(0, 1).reshape(B, T, B_OUT)


def dilated_sample(q, k, v, biases, bufs):
    B, T = q.shape[0], q.shape[1]
    outs = []
    for g, (w, d) in enumerate(DIL_PAIRS):
        buf = bufs[g]
        w_eff = buf.shape[1]
        k_ext = jnp.concatenate([buf[:, :, 0].astype(jnp.float32), k[:, :, gsl(g)]], axis=1)
        v_ext = jnp.concatenate([buf[:, :, 1].astype(jnp.float32), v[:, :, gsl(g)]], axis=1)
        qidx = w_eff + jnp.arange(T)
        outs.append(dilated_group(q[:, :, gsl(g)], k_ext, v_ext, qidx, d, biases[g]))
    return combine_groups(outs).reshape(B, T, B_OUT)


def prompt_mixers(q_a, k_a, v_a, log_a, q_b, k_b, v_b, biases):
    T = q_b.shape[1]
    o_a, S_fin = gla_prompt(q_a, k_a, v_a, log_a)
    o_b = dilated_prompt(q_b, k_b, v_b, biases)
    wins = []
    for g, (w, d) in enumerate(DIL_PAIRS):
        keep = min(w, T)
        wins.append(jnp.stack([k_b[:, T - keep:, gsl(g)], v_b[:, T - keep:, gsl(g)]], axis=2))
    return o_a, o_b, S_fin, wins


def sample_mixers(q_a, k_a, v_a, log_a, q_b, k_b, v_b, biases, S_in, bufs):
    o_a, S_new = gla_sample(q_a, k_a, v_a, log_a, S_in)
    o_b = dilated_sample(q_b, k_b, v_b, biases, bufs)
    rows = [jnp.stack([k_b[:, :, gsl(g)], v_b[:, :, gsl(g)]], axis=2) for g in range(N_GROUPS)]
    return o_a, o_b, S_new, rows


def trunk_layer(x, c, lp, mix_fn):
    B, T = x.shape[0], x.shape[1]
    mod = jax.nn.silu(c.astype(jnp.float32)) @ lp['w_mod'] + lp['b_mod']
    sh1, sc1, gt1, sh2, sc2, gt2 = jnp.split(mod, N_MOD, axis=-1)
    u = modulate(rmsnorm(x, lp['norm1_g']), sh1, sc1)
    gq, gk, gv, gr, glr, bq, bk, bv, gate_a, gate_b = jnp.split(u @ lp['w_in'], in_splits(), axis=-1)
    q_a = gq.reshape(B, T, GLA_HEADS, GLA_DK_HEAD) * (GLA_DK_HEAD ** -0.5)
    k_a = gk.reshape(B, T, GLA_HEADS, GLA_DK_HEAD)
    v_a = gv.reshape(B, T, GLA_HEADS, GLA_DV_HEAD)
    log_a = (jax.nn.log_sigmoid(glr @ lp['w_alpha'] + lp['b_alpha']) / GLA_GATE_NORM
             ).reshape(B, T, GLA_HEADS, GLA_DK_HEAD)
    q_b = rmsnorm(bq.reshape(B, T, B_HEADS, HEAD_DIM), lp['qn_g'])
    k_b = rmsnorm(bk.reshape(B, T, B_HEADS, HEAD_DIM), lp['kn_g'])
    v_b = bv.reshape(B, T, B_HEADS, HEAD_DIM)
    o_a, o_b, s_new, wins = mix_fn(q_a, k_a, v_a, log_a, q_b, k_b, v_b)
    o_a = rmsnorm(o_a, lp['gla_norm_g']).reshape(B, T, GLA_DV) * jax.nn.silu(gr)
    merged = jax.nn.sigmoid(gate_a) * (o_a @ lp['w_pa']) + jax.nn.sigmoid(gate_b) * (o_b @ lp['w_pb'])
    x = x + gt1[:, None, :] * (merged @ lp['w_o'])
    u2 = modulate(rmsnorm(x, lp['norm2_g']), sh2, sc2)
    x = x + gt2[:, None, :] * (jnp.square(jax.nn.relu(u2 @ lp['w_up'])) @ lp['w_down'])
    return x, s_new, wins


def setup_inputs(seed: int = 0) -> dict:
    key = jax.random.key(seed)
    ks = jax.random.split(key, 24)
    n = jax.random.normal
    f32 = jnp.float32
    w_eff = [min(w, PAST_LEN) for (w, d) in DIL_PAIRS]
    inp = {}
    inp['x_prompt'] = n(ks[0], (BATCH, SEQ, D_MODEL), f32)
    inp['x_sample'] = n(ks[1], (DEC_BATCH, DEC_SEQ, D_MODEL), f32)
    inp['c_prompt'] = n(ks[2], (BATCH, D_MODEL), f32)
    inp['c_sample'] = n(ks[3], (DEC_BATCH, D_MODEL), f32)
    inp['state_gla'] = 0.5 * n(ks[4], (DEPTH, DEC_BATCH, GLA_HEADS, GLA_DK_HEAD, GLA_DV_HEAD), f32)
    inp['cache_win1'] = n(ks[5], (DEPTH, DEC_BATCH, w_eff[0], 2, GROUP_HEADS, HEAD_DIM), f32)
    inp['cache_win2'] = n(ks[6], (DEPTH, DEC_BATCH, w_eff[1], 2, GROUP_HEADS, HEAD_DIM), f32)
    inp['cache_win3'] = n(ks[7], (DEPTH, DEC_BATCH, w_eff[2], 2, GROUP_HEADS, HEAD_DIM), f32)
    inp['rel_bias'] = 0.3 * n(ks[8], (REL_BUCKETS, B_HEADS), f32)
    inp['norm1_g'] = 1.0 + 0.02 * n(ks[9], (DEPTH, D_MODEL), f32)
    inp['norm2_g'] = 1.0 + 0.02 * n(ks[10], (DEPTH, D_MODEL), f32)
    inp['w_mod'] = 0.2 * D_MODEL ** -0.5 * n(ks[11], (DEPTH, D_MODEL, N_MOD * D_MODEL), f32)
    inp['b_mod'] = 0.02 * n(ks[12], (DEPTH, N_MOD * D_MODEL), f32)
    inp['w_in'] = D_MODEL ** -0.5 * n(ks[13], (DEPTH, D_MODEL, sum(in_sizes())), f32)
    inp['w_alpha'] = GLA_RANK ** -0.5 * n(ks[14], (DEPTH, GLA_RANK, GLA_DK), f32)
    inp['b_alpha'] = 0.02 * n(ks[15], (DEPTH, GLA_DK), f32)
    inp['gla_norm_g'] = 1.0 + 0.02 * n(ks[16], (DEPTH, GLA_DV_HEAD), f32)
    inp['qn_g'] = 1.0 + 0.02 * n(ks[17], (DEPTH, HEAD_DIM), f32)
    inp['kn_g'] = 1.0 + 0.02 * n(ks[18], (DEPTH, HEAD_DIM), f32)
    inp['w_pa'] = GLA_DV ** -0.5 * n(ks[19], (DEPTH, GLA_DV, D_MODEL), f32)
    inp['w_pb'] = B_OUT ** -0.5 * n(ks[20], (DEPTH, B_OUT, D_MODEL), f32)
    inp['w_o'] = D_MODEL ** -0.5 * n(ks[21], (DEPTH, D_MODEL, D_MODEL), f32)
    inp['w_up'] = D_MODEL ** -0.5 * n(ks[22], (DEPTH, D_MODEL, D_FF), f32)
    inp['w_down'] = D_FF ** -0.5 * n(ks[23], (DEPTH, D_FF, D_MODEL), f32)
    return inp


def reference(x_prompt, x_sample, c_prompt, c_sample, state_gla, cache_win1, cache_win2, cache_win3,
              rel_bias, norm1_g, norm2_g, w_mod, b_mod, w_in, w_alpha, b_alpha, gla_norm_g,
              qn_g, kn_g, w_pa, w_pb, w_o, w_up, w_down):
    biases = group_biases(rel_bias)
    xp = x_prompt.astype(jnp.float32)
    xs = x_sample.astype(jnp.float32)
    gla_p, gla_s = [], []
    win_p = [[] for _ in range(N_GROUPS)]
    win_s = [[] for _ in range(N_GROUPS)]
    for l in range(DEPTH):
        lp = dict(norm1_g=norm1_g[l], norm2_g=norm2_g[l], w_mod=w_mod[l], b_mod=b_mod[l], w_in=w_in[l],
                  w_alpha=w_alpha[l], b_alpha=b_alpha[l], gla_norm_g=gla_norm_g[l], qn_g=qn_g[l],
                  kn_g=kn_g[l], w_pa=w_pa[l], w_pb=w_pb[l], w_o=w_o[l], w_up=w_up[l], w_down=w_down[l])
        p_fn = functools.partial(prompt_mixers, biases=biases)
        s_fn = functools.partial(sample_mixers, biases=biases, S_in=state_gla[l],
                                 bufs=[cache_win1[l], cache_win2[l], cache_win3[l]])
        xp, sp, wp = trunk_layer(xp, c_prompt, lp, p_fn)
        xs, ss, ws = trunk_layer(xs, c_sample, lp, s_fn)
        gla_p.append(sp)
        gla_s.append(ss)
        for g in range(N_GROUPS):
            win_p[g].append(wp[g])
            win_s[g].append(ws[g])
    y_prompt = xp.astype(x_prompt.dtype)
    y_sample = xs.astype(x_sample.dtype)
    return (y_prompt, y_sample, jnp.stack(gla_p), jnp.stack(gla_s),
            jnp.stack(win_p[0]), jnp.stack(win_s[0]), jnp.stack(win_p[1]), jnp.stack(win_s[1]),
            jnp.stack(win_p[2]), jnp.stack(win_s[2]))
```

```python
import functools

import numpy as np
import jax
import jax.numpy as jnp
from jax import lax
from jax.experimental import pallas as pl
from jax.experimental.pallas import tpu as pltpu

F32 = jnp.float32
BF16 = jnp.bfloat16
HIGHEST = lax.Precision.HIGHEST

D_MODEL = 1024
GLA_HEADS = 4
GLA_DK_HEAD = 128
GLA_DV_HEAD = 256
GLA_DK = GLA_HEADS * GLA_DK_HEAD
GLA_DV = GLA_HEADS * GLA_DV_HEAD
GLA_RANK = 16
GLA_GATE_NORM = 16.0
GLA_CHUNK = 64
DIL_PAIRS = ((128, 1), (512, 4), (2048, 16))
N_GROUPS = 3
GROUP_HEADS = 4
HEAD_DIM = 128
B_WIDTH = N_GROUPS * GROUP_HEADS * HEAD_DIM
B_OUT = GROUP_HEADS * HEAD_DIM
TAPS = 129
Q_BLOCK = 128
REL_BUCKETS = 32
REL_MAX_EXACT = 16
REL_MAX_DIST = 2048
D_FF = 4 * D_MODEL
N_MOD = 6
EPS = 1e-6
NEG = -1e30

SEG = 512
NSEG = 19
PROJ_W = NSEG * SEG
SEG_BQ, SEG_BK, SEG_BV = 10, 13, 16
LSE_LANES = 32

VMEM_LIMIT = 48 * 1024 * 1024


def _cparams(*sem):
    return pltpu.CompilerParams(dimension_semantics=sem, vmem_limit_bytes=VMEM_LIMIT)


def _dot(a, b, precision=None):
    return jnp.dot(a, b, preferred_element_type=F32, precision=precision)


def _dot_nt(a, b):
    return lax.dot_general(a, b, (((1,), (1,)), ((), ())), preferred_element_type=F32)


def _dot_tn(a, b, precision=None):
    return lax.dot_general(a, b, (((0,), (0,)), ((), ())), preferred_element_type=F32,
                           precision=precision)


def _sigmoid(x):
    return 1.0 / (1.0 + jnp.exp(-x))


def _norm_mod(x, g, shift, scale):
    ms = jnp.mean(x * x, axis=-1, keepdims=True)
    n = x * lax.rsqrt(ms + EPS) * g
    return n * (1.0 + scale) + shift


def _rms_lanes(x, g):
    ms = jnp.mean(x * x, axis=-1, keepdims=True)
    return x * lax.rsqrt(ms + EPS) * g


def _mod_kernel(c_ref, w_ref, b_ref, o_ref):
    c = c_ref[...]
    a = (c * _sigmoid(c)).astype(BF16)
    o_ref[...] = _dot(a, w_ref[...].astype(BF16)) + b_ref[...]


def _mod_call(c_all, w_mod, b_mod):
    L, D, N = w_mod.shape
    nb = c_all.shape[0]
    tn = 1024
    return pl.pallas_call(
        _mod_kernel,
        grid=(L, N // tn),
        in_specs=[pl.BlockSpec((nb, D), lambda l, n: (0, 0)),
                  pl.BlockSpec((None, D, tn), lambda l, n: (l, 0, n)),
                  pl.BlockSpec((None, 1, tn), lambda l, n: (l, 0, n))],
        out_specs=pl.BlockSpec((None, nb, tn), lambda l, n: (l, 0, n)),
        out_shape=jax.ShapeDtypeStruct((L, nb, N), F32),
        compiler_params=_cparams("arbitrary", "arbitrary"),
        name="adaln_mod",
    )(c_all, w_mod, b_mod.reshape(L, 1, N))


def _inproj_kernel(x_ref, mod_ref, g_ref, w_ref, wglr_ref, wal_ref, bal_ref, qn_ref, kn_ref,
                   proj_ref, la_ref, u_scr):
    s = pl.program_id(2)
    gb, rb, d = x_ref.shape
    tm = gb * rb

    @pl.when(s == 0)
    def _():
        mod = mod_ref[...]
        u = _norm_mod(x_ref[...], g_ref[...], mod[:, 0:1, :], mod[:, 1:2, :])
        ub = u.reshape(tm, d).astype(BF16)
        u_scr[...] = ub
        glr = _dot(ub, wglr_ref[...])
        z = _dot(glr.astype(BF16), wal_ref[...]) + bal_ref[...]
        la = (jnp.minimum(z, 0.0) - jnp.log1p(jnp.exp(-jnp.abs(z)))) * (1.0 / GLA_GATE_NORM)
        la_ref[...] = la.reshape(gb, rb, GLA_DK)

    r = _dot(u_scr[...], w_ref[...])

    def put(v):
        proj_ref[...] = v.reshape(gb, rb, SEG)

    def head_norm(v, g):
        return jnp.concatenate(
            [_rms_lanes(v[:, h * HEAD_DIM:(h + 1) * HEAD_DIM], g) for h in range(SEG // HEAD_DIM)],
            axis=1)

    @pl.when(s == 0)
    def _():
        put(r * (GLA_DK_HEAD ** -0.5))

    @pl.when(((s >= 1) & (s <= 3)) | (s >= SEG_BV))
    def _():
        put(r)

    @pl.when((s == 4) | (s == 5))
    def _():
        put(r * _sigmoid(r))

    @pl.when((s >= 6) & (s <= 9))
    def _():
        put(_sigmoid(r))

    @pl.when((s >= SEG_BQ) & (s < SEG_BK))
    def _():
        put(head_norm(r, qn_ref[...]))

    @pl.when((s >= SEG_BK) & (s < SEG_BV))
    def _():
        put(head_norm(r, kn_ref[...]))


def _inproj_call(x, mod, g, w, wglr, wal, bal, qn, kn, gb, rb):
    G, R, D = x.shape
    tm = gb * rb
    grid = (G // gb, R // rb, NSEG)
    const2 = lambda a, b, s: (0, 0)
    return pl.pallas_call(
        _inproj_kernel,
        grid=grid,
        in_specs=[pl.BlockSpec((gb, rb, D), lambda a, b, s: (a, b, 0)),
                  pl.BlockSpec((gb, N_MOD, D), lambda a, b, s: (a, 0, 0)),
                  pl.BlockSpec((1, D), const2),
                  pl.BlockSpec((D, SEG), lambda a, b, s: (0, s)),
                  pl.BlockSpec((D, 128), const2),
                  pl.BlockSpec((128, GLA_DK), const2),
                  pl.BlockSpec((1, GLA_DK), const2),
                  pl.BlockSpec((1, HEAD_DIM), const2),
                  pl.BlockSpec((1, HEAD_DIM), const2)],
        out_specs=[pl.BlockSpec((gb, rb, SEG), lambda a, b, s: (a, b, s)),
                   pl.BlockSpec((gb, rb, GLA_DK), lambda a, b, s: (a, b, 0))],
        out_shape=[jax.ShapeDtypeStruct((G, R, PROJ_W), F32),
                   jax.ShapeDtypeStruct((G, R, GLA_DK), F32)],
        scratch_shapes=[pltpu.VMEM((tm, D), BF16)],
        compiler_params=_cparams("arbitrary", "arbitrary", "arbitrary"),
        name="in_proj",
    )(x, mod, g, w, wglr, wal, bal, qn, kn)


def _gla_head_chunk(q, k, v, la, S, tril, lmat, ones_c):
    b = _dot(lmat, la, HIGHEST)
    qe = (q * jnp.exp(b)).astype(BF16)
    ke = (k * jnp.exp(-b)).astype(BF16)
    vb = v.astype(BF16)
    sc = jnp.where(tril, _dot_nt(qe, ke), 0.0)
    o = _dot(qe, S.astype(BF16)) + _dot(sc.astype(BF16), vb)
    b_last = b[b.shape[0] - 1:, :]
    kd = (k * jnp.exp(b_last - b)).astype(BF16)
    dcol = _dot_tn(la, ones_c, HIGHEST)
    dec = jnp.exp(dcol)
    s_new = S * jnp.concatenate([dec, dec], axis=1) + _dot_tn(kd, vb)
    return o, s_new


def _gla_prompt_kernel(q_ref, k_ref, v_ref, gr_ref, la_ref, gn_ref, o_ref, sfin_ref, s_scr):
    j = pl.program_id(1)
    tb = q_ref.shape[0]
    C = GLA_CHUNK

    @pl.when(j == 0)
    def _():
        s_scr[...] = jnp.zeros_like(s_scr)

    row = lax.broadcasted_iota(jnp.int32, (C, C), 0)
    col = lax.broadcasted_iota(jnp.int32, (C, C), 1)
    tril = col <= row
    lmat = tril.astype(F32)
    ones_c = jnp.ones((C, 128), F32)
    gn = gn_ref[...]

    def body(c, carry):
        r0 = pl.multiple_of(c * C, C)
        rows = pl.ds(r0, C)
        for h in range(GLA_HEADS):
            ksl = slice(h * GLA_DK_HEAD, (h + 1) * GLA_DK_HEAD)
            vsl = slice(h * GLA_DV_HEAD, (h + 1) * GLA_DV_HEAD)
            o, s_new = _gla_head_chunk(q_ref[rows, ksl], k_ref[rows, ksl], v_ref[rows, vsl],
                                       la_ref[rows, ksl], s_scr[h], tril, lmat, ones_c)
            s_scr[h] = s_new
            o_ref[rows, vsl] = _rms_lanes(o, gn) * gr_ref[rows, vsl]
        return carry

    lax.fori_loop(0, tb // C, body, 0)

    @pl.when(j == pl.num_programs(1) - 1)
    def _():
        sfin_ref[...] = s_scr[...]


def _gla_prompt_call(proj, la, gn, tb):
    B, T, _ = proj.shape
    return pl.pallas_call(
        _gla_prompt_kernel,
        grid=(B, T // tb),
        in_specs=[pl.BlockSpec((None, tb, GLA_DK), lambda b, j: (b, j, 0)),
                  pl.BlockSpec((None, tb, GLA_DK), lambda b, j: (b, j, 1)),
                  pl.BlockSpec((None, tb, GLA_DV), lambda b, j: (b, j, 1)),
                  pl.BlockSpec((None, tb, GLA_DV), lambda b, j: (b, j, 2)),
                  pl.BlockSpec((None, tb, GLA_DK), lambda b, j: (b, j, 0)),
                  pl.BlockSpec((1, GLA_DV_HEAD), lambda b, j: (0, 0))],
        out_specs=[pl.BlockSpec((None, tb, GLA_DV), lambda b, j: (b, j, 0)),
                   pl.BlockSpec((None, GLA_HEADS, GLA_DK_HEAD, GLA_DV_HEAD),
                                lambda b, j: (b, 0, 0, 0))],
        out_shape=[jax.ShapeDtypeStruct((B, T, GLA_DV), F32),
                   jax.ShapeDtypeStruct((B, GLA_HEADS, GLA_DK_HEAD, GLA_DV_HEAD), F32)],
        scratch_shapes=[pltpu.VMEM((GLA_HEADS, GLA_DK_HEAD, GLA_DV_HEAD), F32)],
        compiler_params=_cparams("arbitrary", "arbitrary"),
        name="gla_prompt",
    )(proj, proj, proj, proj, la, gn)


def _gla_sample_kernel(q_ref, k_ref, v_ref, gr_ref, la_ref, sin_ref, gn_ref, o_ref, sout_ref):
    sb, ts, _ = q_ref.shape
    R = sb * ts
    row = lax.broadcasted_iota(jnp.int32, (R, R), 0)
    col = lax.broadcasted_iota(jnp.int32, (R, R), 1)
    shift = ts.bit_length() - 1
    same = jnp.right_shift(row, shift) == jnp.right_shift(col, shift)
    tril = same & (col <= row)
    lmat = tril.astype(F32)
    bones = same.astype(F32)
    ones_t = jnp.ones((ts, 128), F32)
    gn = gn_ref[...]
    for h in range(GLA_HEADS):
        ksl = slice(h * GLA_DK_HEAD, (h + 1) * GLA_DK_HEAD)
        vsl = slice(h * GLA_DV_HEAD, (h + 1) * GLA_DV_HEAD)
        q = q_ref[:, :, ksl].reshape(R, GLA_DK_HEAD)
        k = k_ref[:, :, ksl].reshape(R, GLA_DK_HEAD)
        v = v_ref[:, :, vsl].reshape(R, GLA_DV_HEAD)
        la = la_ref[:, :, ksl].reshape(R, GLA_DK_HEAD)
        b = _dot(lmat, la, HIGHEST)
        btot = _dot(bones, la, HIGHEST)
        qe = q * jnp.exp(b)
        ke = (k * jnp.exp(-b)).astype(BF16)
        kd = k * jnp.exp(btot - b)
        vb = v.astype(BF16)
        sc = jnp.where(tril, _dot_nt(qe.astype(BF16), ke), 0.0)
        o_intra = _dot(sc.astype(BF16), vb)
        for i in range(sb):
            rs = slice(i * ts, (i + 1) * ts)
            S = sin_ref[i, h]
            o = _dot(qe[rs].astype(BF16), S.astype(BF16)) + o_intra[rs]
            kv = _dot_tn(kd[rs].astype(BF16), v[rs].astype(BF16))
            dec = jnp.exp(_dot_tn(la[rs], ones_t, HIGHEST))
            sout_ref[i, h] = S * jnp.concatenate([dec, dec], axis=1) + kv
            o_ref[i, :, vsl] = _rms_lanes(o, gn) * gr_ref[i, :, vsl]


def _gla_sample_call(proj, la, state, l, gn, sb):
    NS, TS, _ = proj.shape
    st_shape = (sb, GLA_HEADS, GLA_DK_HEAD, GLA_DV_HEAD)
    st_spec = pl.BlockSpec(st_shape, lambda i: (i, 0, 0, 0))
    return pl.pallas_call(
        _gla_sample_kernel,
        grid=(NS // sb,),
        in_specs=[pl.BlockSpec((sb, TS, GLA_DK), lambda i: (i, 0, 0)),
                  pl.BlockSpec((sb, TS, GLA_DK), lambda i: (i, 0, 1)),
                  pl.BlockSpec((sb, TS, GLA_DV), lambda i: (i, 0, 1)),
                  pl.BlockSpec((sb, TS, GLA_DV), lambda i: (i, 0, 2)),
                  pl.BlockSpec((sb, TS, GLA_DK), lambda i: (i, 0, 0)),
                  pl.BlockSpec((None,) + st_shape, lambda i: (l, i, 0, 0, 0)),
                  pl.BlockSpec((1, GLA_DV_HEAD), lambda i: (0, 0))],
        out_specs=[pl.BlockSpec((sb, TS, GLA_DV), lambda i: (i, 0, 0)), st_spec],
        out_shape=[jax.ShapeDtypeStruct((NS, TS, GLA_DV), F32),
                   jax.ShapeDtypeStruct(state.shape[1:], F32)],
        compiler_params=_cparams("arbitrary"),
        name="gla_sample",
    )(proj, proj, proj, proj, la, state, gn)


def _dil_prompt_kernel(q_ref, kp_ref, kc_ref, vp_ref, vc_ref, bias_ref, o_ref, lse_ref):
    first = pl.program_id(2) == 0
    neg_prev = jnp.where(first, NEG, 0.0)
    scale = HEAD_DIM ** -0.5
    for h in range(GROUP_HEADS):
        sl = slice(h * HEAD_DIM, (h + 1) * HEAD_DIM)
        q = q_ref[:, sl].astype(BF16)
        lp = _dot_nt(q, kp_ref[:, sl].astype(BF16)) * scale + bias_ref[h, :, 0:Q_BLOCK] + neg_prev
        lc = _dot_nt(q, kc_ref[:, sl].astype(BF16)) * scale + bias_ref[h, :, Q_BLOCK:2 * Q_BLOCK]
        m = jnp.maximum(jnp.max(lp, axis=-1, keepdims=True), jnp.max(lc, axis=-1, keepdims=True))
        pp = jnp.exp(lp - m)
        pc = jnp.exp(lc - m)
        s = jnp.sum(pp, axis=-1, keepdims=True) + jnp.sum(pc, axis=-1, keepdims=True)
        acc = (_dot(pp.astype(BF16), vp_ref[:, sl].astype(BF16))
               + _dot(pc.astype(BF16), vc_ref[:, sl].astype(BF16)))
        o_ref[:, sl] = acc / s
        lse_ref[:, h * LSE_LANES:(h + 1) * LSE_LANES] = jnp.broadcast_to(
            m + jnp.log(s), (Q_BLOCK, LSE_LANES))


def _dil_prompt_call(proj, bias_g, g, d):
    B, T, _ = proj.shape
    n = T // d
    pv = proj.reshape(B, n, d * PROJ_W)
    nblk = n // Q_BLOCK

    def spec(seg, prev):
        if prev:
            return pl.BlockSpec((None, Q_BLOCK, SEG),
                                lambda b, p, j: (b, jnp.maximum(j - 1, 0), p * NSEG + seg))
        return pl.BlockSpec((None, Q_BLOCK, SEG), lambda b, p, j: (b, j, p * NSEG + seg))

    o, lse = pl.pallas_call(
        _dil_prompt_kernel,
        grid=(B, d, nblk),
        in_specs=[spec(SEG_BQ + g, False),
                  spec(SEG_BK + g, True), spec(SEG_BK + g, False),
                  spec(SEG_BV + g, True), spec(SEG_BV + g, False),
                  pl.BlockSpec((GROUP_HEADS, Q_BLOCK, 2 * Q_BLOCK), lambda b, p, j: (0, 0, 0))],
        out_specs=[pl.BlockSpec((None, Q_BLOCK, B_OUT), lambda b, p, j: (b, j, p)),
                   pl.BlockSpec((None, Q_BLOCK, 128), lambda b, p, j: (b, j, p))],
        out_shape=[jax.ShapeDtypeStruct((B, n, d * B_OUT), F32),
                   jax.ShapeDtypeStruct((B, n, d * 128), F32)],
        compiler_params=_cparams("arbitrary", "arbitrary", "arbitrary"),
        name="dilated_prompt_g%d" % g,
    )(pv, pv, pv, pv, pv, bias_g)
    return o.reshape(B, T, B_OUT), lse.reshape(B, T, 128)


def _dil_sample_kernel(q0_ref, q1_ref, q2_ref, k0_ref, k1_ref, k2_ref, v0_ref, v1_ref, v2_ref,
                       c0_ref, c1_ref, c2_ref, bias_ref, o_ref):
    ts = q0_ref.shape[0]
    scale = HEAD_DIM ** -0.5
    q_refs = (q0_ref, q1_ref, q2_ref)
    k_refs = (k0_ref, k1_ref, k2_ref)
    v_refs = (v0_ref, v1_ref, v2_ref)
    n2 = c2_ref.shape[0] * c2_ref.shape[1]
    widths = (c0_ref.shape[0], c1_ref.shape[0], n2)
    zpad = jnp.zeros((128 - ts, HEAD_DIM), F32)
    for h in range(GROUP_HEADS):
        sl = slice(h * HEAD_DIM, (h + 1) * HEAD_DIM)
        vl = slice(B_OUT + h * HEAD_DIM, B_OUT + (h + 1) * HEAD_DIM)
        keys = (c0_ref[:, sl], c1_ref[:, sl], c2_ref[:, :, sl].reshape(n2, HEAD_DIM))
        vals = (c0_ref[:, vl], c1_ref[:, vl], c2_ref[:, :, vl].reshape(n2, HEAD_DIM))
        logits = []
        values = []
        off = 0
        for g in range(N_GROUPS):
            q = q_refs[g][:, sl].astype(BF16)
            lg = _dot_nt(q, keys[g].astype(BF16)) * scale + bias_ref[h, :, off:off + widths[g]]
            logits.append(lg)
            values.append(vals[g].astype(BF16))
            off += widths[g]
        for g in range(N_GROUPS):
            q = q_refs[g][:, sl].astype(BF16)
            kn = jnp.concatenate([k_refs[g][:, sl], zpad], axis=0).astype(BF16)
            vn = jnp.concatenate([v_refs[g][:, sl], zpad], axis=0).astype(BF16)
            lg = _dot_nt(q, kn) * scale + bias_ref[h, :, off:off + 128]
            logits.append(lg)
            values.append(vn)
            off += 128
        m = logits[0].max(axis=-1, keepdims=True)
        for lg in logits[1:]:
            m = jnp.maximum(m, lg.max(axis=-1, keepdims=True))
        s = jnp.zeros((ts, 1), F32)
        acc = jnp.zeros((ts, HEAD_DIM), F32)
        for lg, vv in zip(logits, values):
            p = jnp.exp(lg - m)
            s = s + jnp.sum(p, axis=-1, keepdims=True)
            acc = acc + _dot(p.astype(BF16), vv)
        o_ref[:, sl] = acc / s


def _dil_sample_call(proj, caches, l, bias_s):
    NS, TS, _ = proj.shape
    c0, c1, c2 = caches
    w0, w1, n2, d2, row = c0.shape[2], c1.shape[2], c2.shape[2], c2.shape[3], c2.shape[4]

    def pspec(seg):
        return pl.BlockSpec((None, TS, SEG), lambda i: (i, 0, seg))

    return pl.pallas_call(
        _dil_sample_kernel,
        grid=(NS,),
        in_specs=[pspec(SEG_BQ), pspec(SEG_BQ + 1), pspec(SEG_BQ + 2),
                  pspec(SEG_BK), pspec(SEG_BK + 1), pspec(SEG_BK + 2),
                  pspec(SEG_BV), pspec(SEG_BV + 1), pspec(SEG_BV + 2),
                  pl.BlockSpec((None, None, w0, row), lambda i: (l, i, 0, 0)),
                  pl.BlockSpec((None, None, w1, row), lambda i: (l, i, 0, 0)),
                  pl.BlockSpec((None, None, n2, TS, row), lambda i: (l, i, 0, 0, 0)),
                  pl.BlockSpec(bias_s.shape, lambda i: (0, 0, 0))],
        out_specs=pl.BlockSpec((None, TS, B_OUT), lambda i: (i, 0, 0)),
        out_shape=jax.ShapeDtypeStruct((NS, TS, B_OUT), F32),
        compiler_params=_cparams("arbitrary"),
        name="dilated_sample",
    )(proj, proj, proj, proj, proj, proj, proj, proj, proj, c0, c1, c2, bias_s)


def _merge_kernel(*refs, combine):
    if combine:
        (x_ref, mod_ref, oa_ref, ga_ref, gb_ref, o0_ref, o1_ref, o2_ref, l0_ref, l1_ref, l2_ref,
         wpa_ref, wpb_ref, wo_ref, out_ref) = refs
    else:
        (x_ref, mod_ref, oa_ref, ga_ref, gb_ref, ob_ref, wpa_ref, wpb_ref, wo_ref, out_ref) = refs
    gb, rb, d = x_ref.shape
    tm = gb * rb
    if combine:
        o_refs = (o0_ref, o1_ref, o2_ref)
        l_refs = (l0_ref, l1_ref, l2_ref)
        parts = []
        for h in range(GROUP_HEADS):
            sl = slice(h * HEAD_DIM, (h + 1) * HEAD_DIM)
            ls = [l[:, :, h * LSE_LANES:h * LSE_LANES + 1].reshape(tm, 1) for l in l_refs]
            m = jnp.maximum(jnp.maximum(ls[0], ls[1]), ls[2])
            es = [jnp.exp(l - m) for l in ls]
            den = es[0] + es[1] + es[2]
            parts.append(sum((es[g] / den) * o_refs[g][:, :, sl].reshape(tm, HEAD_DIM)
                             for g in range(N_GROUPS)))
        ob = jnp.concatenate(parts, axis=1)
    else:
        ob = ob_ref[...].reshape(tm, B_OUT)
    oa = oa_ref[...].reshape(tm, GLA_DV)
    m1 = _dot(oa.astype(BF16), wpa_ref[...])
    m2 = _dot(ob.astype(BF16), wpb_ref[...])
    merged = ga_ref[...].reshape(tm, d) * m1 + gb_ref[...].reshape(tm, d) * m2
    y = _dot(merged.astype(BF16), wo_ref[...]).reshape(gb, rb, d)
    out_ref[...] = x_ref[...] + mod_ref[...][:, 2:3, :] * y


def _merge_call(x, mod, oa, proj, ob_parts, wpa, wpb, wo, gb, rb):
    G, R, D = x.shape
    combine = len(ob_parts) > 1
    tok = lambda w, c: pl.BlockSpec((gb, rb, w), lambda a, b: (a, b, c))
    const = lambda shp: pl.BlockSpec(shp, lambda a, b: (0, 0))
    in_specs = [tok(D, 0), pl.BlockSpec((gb, N_MOD, D), lambda a, b: (a, 0, 0)),
                tok(GLA_DV, 0), tok(D, 3), tok(D, 4)]
    args = [x, mod, oa, proj, proj]
    if combine:
        os_, ls_ = ob_parts
        in_specs += [tok(B_OUT, 0)] * 3 + [tok(128, 0)] * 3
        args += list(os_) + list(ls_)
    else:
        in_specs += [tok(B_OUT, 0)]
        args += list(ob_parts)
    in_specs += [const((GLA_DV, D)), const((B_OUT, D)), const((D, D))]
    args += [wpa, wpb, wo]
    return pl.pallas_call(
        functools.partial(_merge_kernel, combine=combine),
        grid=(G // gb, R // rb),
        in_specs=in_specs,
        out_specs=tok(D, 0),
        out_shape=jax.ShapeDtypeStruct((G, R, D), F32),
        compiler_params=_cparams("arbitrary", "arbitrary"),
        name="merge_combine" if combine else "merge",
    )(*args)


def _mlp_kernel(x_ref, mod_ref, g_ref, wu_ref, wd_ref, out_ref, u_scr, acc_scr):
    f = pl.program_id(2)
    gb, rb, d = x_ref.shape
    tm = gb * rb

    @pl.when(f == 0)
    def _():
        mod = mod_ref[...]
        u = _norm_mod(x_ref[...], g_ref[...], mod[:, 3:4, :], mod[:, 4:5, :])
        u_scr[...] = u.reshape(tm, d).astype(BF16)
        acc_scr[...] = jnp.zeros_like(acc_scr)

    hid = jnp.maximum(_dot(u_scr[...], wu_ref[...]), 0.0)
    acc_scr[...] += _dot((hid * hid).astype(BF16), wd_ref[...])

    @pl.when(f == pl.num_programs(2) - 1)
    def _():
        out_ref[...] = x_ref[...] + mod_ref[...][:, 5:6, :] * acc_scr[...].reshape(gb, rb, d)


def _mlp_call(x, mod, g, wu, wd, gb, rb, tf):
    G, R, D = x.shape
    F = wu.shape[1]
    tm = gb * rb
    return pl.pallas_call(
        _mlp_kernel,
        grid=(G // gb, R // rb, F // tf),
        in_specs=[pl.BlockSpec((gb, rb, D), lambda a, b, f: (a, b, 0)),
                  pl.BlockSpec((gb, N_MOD, D), lambda a, b, f: (a, 0, 0)),
                  pl.BlockSpec((1, D), lambda a, b, f: (0, 0)),
                  pl.BlockSpec((D, tf), lambda a, b, f: (0, f)),
                  pl.BlockSpec((tf, D), lambda a, b, f: (f, 0))],
        out_specs=pl.BlockSpec((gb, rb, D), lambda a, b, f: (a, b, 0)),
        out_shape=jax.ShapeDtypeStruct((G, R, D), F32),
        scratch_shapes=[pltpu.VMEM((tm, D), BF16), pltpu.VMEM((tm, D), F32)],
        compiler_params=_cparams("arbitrary", "arbitrary", "arbitrary"),
        name="mlp",
    )(x, mod, g, wu, wd)


def _t5_bucket(dist):
    dist = np.asarray(dist)
    large = REL_MAX_EXACT + (np.log(np.maximum(dist, 1) / REL_MAX_EXACT)
                             / np.log(REL_MAX_DIST / REL_MAX_EXACT)
                             * (REL_BUCKETS - REL_MAX_EXACT)).astype(np.int32)
    large = np.minimum(large, REL_BUCKETS - 1)
    return np.where(dist < REL_MAX_EXACT, dist, large).astype(np.int32)


def _tap_bias(rel_bias, g):
    d = DIL_PAIRS[g][1]
    idx = _t5_bucket(d * np.arange(TAPS))
    return rel_bias[idx][:, g * GROUP_HEADS:(g + 1) * GROUP_HEADS].astype(F32)


def _bias_table(tb, tap, valid):
    t = jnp.transpose(tb[np.clip(tap, 0, TAPS - 1)], (2, 0, 1))
    return jnp.where(jnp.asarray(valid)[None], t, NEG)


def _prompt_bias(rel_bias, g):
    i = np.arange(Q_BLOCK)[:, None]
    kk = np.arange(2 * Q_BLOCK)[None, :]
    tap = i + Q_BLOCK - kk
    valid = (tap >= 0) & (tap < TAPS)
    return _bias_table(_tap_bias(rel_bias, g), tap, valid)


def _sample_bias(rel_bias, ts, widths):
    t = np.arange(ts)[:, None]
    cache_parts, new_parts = [], []
    for g, (w, d) in enumerate(DIL_PAIRS):
        tb = _tap_bias(rel_bias, g)
        w_eff = widths[g]
        if g == 2:
            r = np.arange((w_eff // d) * ts)[None, :]
            e = d * (r // ts) + (r % ts)
        else:
            e = np.arange(w_eff)[None, :]
        dist = w_eff + t - e
        tap = dist // d
        valid = (dist % d == 0) & (tap >= 0) & (tap < TAPS)
        cache_parts.append(_bias_table(tb, tap, valid))
        tn = np.arange(128)[None, :]
        dist = t - tn
        tap = dist // d
        valid = (tn < ts) & (dist >= 0) & (dist % d == 0) & (tap < TAPS)
        new_parts.append(_bias_table(tb, tap, valid))
    return jnp.concatenate(cache_parts + new_parts, axis=-1)


def kernel(x_prompt, x_sample, c_prompt, c_sample, state_gla, cache_win1, cache_win2, cache_win3,
           rel_bias, norm1_g, norm2_g, w_mod, b_mod, w_in, w_alpha, b_alpha, gla_norm_g,
           qn_g, kn_g, w_pa, w_pb, w_o, w_up, w_down):
    B, T, D = x_prompt.shape
    NS, TS, _ = x_sample.shape
    L = w_mod.shape[0]
    widths = [c.shape[2] for c in (cache_win1, cache_win2, cache_win3)]
    d2 = DIL_PAIRS[2][1]
    row = 2 * B_OUT
    caches = (cache_win1.astype(F32).reshape(L, NS, widths[0], row),
              cache_win2.astype(F32).reshape(L, NS, widths[1], row),
              cache_win3.astype(F32).reshape(L, NS, widths[2] // d2, d2, row))
    state = state_gla.astype(F32)
    assert D == D_MODEL and TS == 8 and T % (DIL_PAIRS[2][1] * Q_BLOCK) == 0
    assert cache_win3.shape[2] % DIL_PAIRS[2][1] == 0 and TS <= DIL_PAIRS[2][1]

    mods = _mod_call(jnp.concatenate([c_prompt, c_sample], axis=0).astype(F32), w_mod, b_mod)
    mods = mods.reshape(L, B + NS, N_MOD, D)

    sizes = (GLA_DK, GLA_DK, GLA_DV, GLA_DV, GLA_RANK, B_WIDTH, B_WIDTH, B_WIDTH, D_MODEL, D_MODEL)
    offs = np.concatenate([[0], np.cumsum(sizes)])
    piece = lambda i: w_in[:, :, offs[i]:offs[i + 1]]
    w_main = jnp.concatenate([piece(i) for i in (0, 1, 2, 3, 8, 9, 5, 6, 7)], axis=-1).astype(BF16)
    w_glr = jnp.pad(piece(4), ((0, 0), (0, 0), (0, 128 - GLA_RANK))).astype(BF16)
    w_al = jnp.pad(w_alpha, ((0, 0), (0, 128 - GLA_RANK), (0, 0))).astype(BF16)
    w_pa_b, w_pb_b, w_o_b = w_pa.astype(BF16), w_pb.astype(BF16), w_o.astype(BF16)
    w_up_b, w_down_b = w_up.astype(BF16), w_down.astype(BF16)

    bias_p = [_prompt_bias(rel_bias, g) for g in range(N_GROUPS)]
    bias_s = _sample_bias(rel_bias, TS, widths)

    tm = 1024
    xp = x_prompt.astype(F32)
    xs = x_sample.astype(F32)
    gla_p, gla_s = [], []
    win_p = [[] for _ in range(N_GROUPS)]
    win_s = [[] for _ in range(N_GROUPS)]
    row2 = lambda v: v.reshape(1, -1)
    for l in range(L):
        mod_p, mod_s = mods[l, :B], mods[l, B:]
        common = (row2(norm1_g[l]), w_main[l], w_glr[l], w_al[l], row2(b_alpha[l]),
                  row2(qn_g[l]), row2(kn_g[l]))
        gn = row2(gla_norm_g[l])

        proj, la = _inproj_call(xp, mod_p, *common, gb=1, rb=tm)
        oa, s_fin = _gla_prompt_call(proj, la, gn, tb=512)
        outs = [_dil_prompt_call(proj, bias_p[g], g, DIL_PAIRS[g][1]) for g in range(N_GROUPS)]
        xp = _merge_call(xp, mod_p, oa, proj, ([o for o, _ in outs], [s for _, s in outs]),
                         w_pa_b[l], w_pb_b[l], w_o_b[l], gb=1, rb=tm // 2)
        xp = _mlp_call(xp, mod_p, row2(norm2_g[l]), w_up_b[l], w_down_b[l], gb=1, rb=tm, tf=1024)
        gla_p.append(s_fin)
        for g, (w, d) in enumerate(DIL_PAIRS):
            keep = min(w, T)
            kseg = proj[:, T - keep:, (SEG_BK + g) * SEG:(SEG_BK + g + 1) * SEG]
            vseg = proj[:, T - keep:, (SEG_BV + g) * SEG:(SEG_BV + g + 1) * SEG]
            win_p[g].append(jnp.stack([kseg, vseg], axis=2).reshape(
                B, keep, 2, GROUP_HEADS, HEAD_DIM))

        proj, la = _inproj_call(xs, mod_s, *common, gb=NS, rb=TS)
        oa, s_new = _gla_sample_call(proj, la, state, l, gn, sb=8)
        ob = _dil_sample_call(proj, caches, l, bias_s)
        xs = _merge_call(xs, mod_s, oa, proj, (ob,), w_pa_b[l], w_pb_b[l], w_o_b[l], gb=NS, rb=TS)
        xs = _mlp_call(xs, mod_s, row2(norm2_g[l]), w_up_b[l], w_down_b[l], gb=NS, rb=TS, tf=1024)
        gla_s.append(s_new)
        for g in range(N_GROUPS):
            kseg = proj[:, :, (SEG_BK + g) * SEG:(SEG_BK + g + 1) * SEG]
            vseg = proj[:, :, (SEG_BV + g) * SEG:(SEG_BV + g + 1) * SEG]
            win_s[g].append(jnp.stack([kseg, vseg], axis=2).reshape(
                NS, TS, 2, GROUP_HEADS, HEAD_DIM))

    return (xp.astype(x_prompt.dtype), xs.astype(x_sample.dtype), jnp.stack(gla_p), jnp.stack(gla_s),
            jnp.stack(win_p[0]), jnp.stack(win_s[0]), jnp.stack(win_p[1]), jnp.stack(win_s[1]),
            jnp.stack(win_p[2]), jnp.stack(win_s[2]))
```

```python
import functools

import numpy as np
import jax
import jax.numpy as jnp
from jax import lax
from jax.experimental import pallas as pl
from jax.experimental.pallas import tpu as pltpu

F32 = jnp.float32
BF16 = jnp.bfloat16

D_MODEL = 1024
GLA_HEADS = 4
GLA_DK_HEAD = 128
GLA_DV_HEAD = 256
GLA_DK = GLA_HEADS * GLA_DK_HEAD
GLA_DV = GLA_HEADS * GLA_DV_HEAD
GLA_RANK = 16
GLA_GATE_NORM = 16.0
GLA_CHUNK = 64
DIL_PAIRS = ((128, 1), (512, 4), (2048, 16))
N_GROUPS = 3
GROUP_HEADS = 4
HEAD_DIM = 128
B_WIDTH = N_GROUPS * GROUP_HEADS * HEAD_DIM
B_OUT = GROUP_HEADS * HEAD_DIM
TAPS = 129
Q_BLOCK = 128
REL_BUCKETS = 32
REL_MAX_EXACT = 16
REL_MAX_DIST = 2048
D_FF = 4 * D_MODEL
N_MOD = 6
EPS = 1e-6
NEG = -1e30

LANES = 128
CACHE_ROW = 2 * GROUP_HEADS

SEG = 512
SEG_MAIN = 10
NSEG = 19
MAIN_W = SEG_MAIN * SEG
SEG_BQ, SEG_BK, SEG_BV = 10, 13, 16
N_QKV = (NSEG - SEG_MAIN) * GROUP_HEADS
SUPER = 2048

VMEM_LIMIT = 48 * 1024 * 1024


def _cparams(*sem):
    return pltpu.CompilerParams(dimension_semantics=sem, vmem_limit_bytes=VMEM_LIMIT)


def _dot(a, b):
    return jnp.dot(a, b, preferred_element_type=F32)


def _dot_nt(a, b):
    return lax.dot_general(a, b, (((1,), (1,)), ((), ())), preferred_element_type=F32)


def _dot_tn(a, b):
    return lax.dot_general(a, b, (((0,), (0,)), ((), ())), preferred_element_type=F32)


def _sigmoid(x):
    return 1.0 / (1.0 + jnp.exp(-x))


def _norm_mod(x, g, shift, scale):
    ms = jnp.mean(x * x, axis=-1, keepdims=True)
    n = x * lax.rsqrt(ms + EPS) * g
    return n * (1.0 + scale) + shift


def _rms_lanes(x, g):
    ms = jnp.mean(x * x, axis=-1, keepdims=True)
    return x * lax.rsqrt(ms + EPS) * g


def _split2(x):
    hi = x.astype(BF16)
    lo = (x - hi.astype(F32)).astype(BF16)
    return hi, lo


def _mod_kernel(c_ref, w_ref, b_ref, o_ref):
    c = c_ref[...]
    a = (c * _sigmoid(c)).astype(BF16)
    o_ref[...] = _dot(a, w_ref[...].astype(BF16)) + b_ref[...]


def _mod_call(c_all, w_mod, b_mod):
    L, D, N = w_mod.shape
    nb = c_all.shape[0]
    tn = 1024
    return pl.pallas_call(
        _mod_kernel,
        grid=(L, N // tn),
        in_specs=[pl.BlockSpec((nb, D), lambda l, n: (0, 0)),
                  pl.BlockSpec((None, D, tn), lambda l, n: (l, 0, n)),
                  pl.BlockSpec((None, 1, tn), lambda l, n: (l, 0, n))],
        out_specs=pl.BlockSpec((None, nb, tn), lambda l, n: (l, 0, n)),
        out_shape=jax.ShapeDtypeStruct((L, nb, N), F32),
        compiler_params=_cparams("arbitrary", "arbitrary"),
        name="adaln_mod",
    )(c_all, w_mod, b_mod.reshape(L, 1, N))


def _inproj_kernel(x_ref, mod_ref, g_ref, w_ref, wglr_ref, wal_ref, bal_ref, qn_ref, kn_ref,
                   proj_ref, la_ref, qkv_ref, u_scr):
    s = pl.program_id(2)
    gb, rb, d = x_ref.shape
    tm = gb * rb
    rc = min(tm, 512)

    @pl.when(s == 0)
    def _():
        mod = mod_ref[...]
        u = _norm_mod(x_ref[...], g_ref[...], mod[:, 0:1, :], mod[:, 1:2, :])
        ub = u.reshape(tm, d).astype(BF16)
        u_scr[...] = ub
        glr = _dot(ub, wglr_ref[...])
        z = _dot(glr.astype(BF16), wal_ref[...]) + bal_ref[...]
        la = (jnp.minimum(z, 0.0) - jnp.log1p(jnp.exp(-jnp.abs(z)))) * (1.0 / GLA_GATE_NORM)
        la_ref[...] = la.reshape(gb, rb, GLA_DK)

    @pl.when(s < SEG_MAIN)
    def _():
        proj_ref[...] = _dot(u_scr[...], w_ref[...]).reshape(gb, rb, SEG)

    def to_heads(r0, rows, val):
        for h in range(GROUP_HEADS):
            qkv_ref[h, r0:r0 + rows, :] = val[:, h * HEAD_DIM:(h + 1) * HEAD_DIM]

    def normed(gain_ref):
        g = gain_ref[...]
        for c in range(tm // rc):
            r = _dot(u_scr[c * rc:(c + 1) * rc, :], w_ref[...])
            to_heads(c * rc, rc, jnp.concatenate(
                [_rms_lanes(r[:, h * HEAD_DIM:(h + 1) * HEAD_DIM], g) for h in range(GROUP_HEADS)],
                axis=1))

    @pl.when((s >= SEG_BQ) & (s < SEG_BK))
    def _():
        normed(qn_ref)

    @pl.when((s >= SEG_BK) & (s < SEG_BV))
    def _():
        normed(kn_ref)

    @pl.when(s >= SEG_BV)
    def _():
        to_heads(0, tm, _dot(u_scr[...], w_ref[...]))


def _inproj_call(x, mod, g, w, wglr, wal, bal, qn, kn, gb, rb):
    G, R, D = x.shape
    tm = gb * rb
    na, nr = G // gb, R // rb
    const2 = lambda a, b, s: (0, 0)
    return pl.pallas_call(
        _inproj_kernel,
        grid=(na, nr, NSEG),
        in_specs=[pl.BlockSpec((gb, rb, D), lambda a, b, s: (a, b, 0)),
                  pl.BlockSpec((gb, N_MOD, D), lambda a, b, s: (a, 0, 0)),
                  pl.BlockSpec((1, D), const2),
                  pl.BlockSpec((D, SEG), lambda a, b, s: (0, s)),
                  pl.BlockSpec((D, LANES), const2),
                  pl.BlockSpec((LANES, GLA_DK), const2),
                  pl.BlockSpec((1, GLA_DK), const2),
                  pl.BlockSpec((1, HEAD_DIM), const2),
                  pl.BlockSpec((1, HEAD_DIM), const2)],
        out_specs=[pl.BlockSpec((gb, rb, SEG), lambda a, b, s: (a, b, jnp.minimum(s, SEG_MAIN - 1))),
                   pl.BlockSpec((gb, rb, GLA_DK), lambda a, b, s: (a, b, 0)),
                   pl.BlockSpec((None, GROUP_HEADS, tm, HEAD_DIM),
                                lambda a, b, s: (a, jnp.clip(s - SEG_BQ, 0, NSEG - SEG_BQ - 1), b, 0))],
        out_shape=[jax.ShapeDtypeStruct((G, R, MAIN_W), F32),
                   jax.ShapeDtypeStruct((G, R, GLA_DK), F32),
                   jax.ShapeDtypeStruct((na, N_QKV, nr * tm, HEAD_DIM), F32)],
        scratch_shapes=[pltpu.VMEM((tm, D), BF16)],
        compiler_params=_cparams("arbitrary", "arbitrary", "arbitrary"),
        name="in_proj",
    )(x, mod, g, w, wglr, wal, bal, qn, kn)


def _gla_unit(q, k, v, la, S, tril, lmat, ones_c):
    la_hi, la_lo = _split2(la)
    b = _dot(lmat, la_hi) + _dot(lmat, la_lo)
    qe = (q * (GLA_DK_HEAD ** -0.5) * jnp.exp(b)).astype(BF16)
    ke = (k * jnp.exp(-b)).astype(BF16)
    vb = v.astype(BF16)
    sc = jnp.where(tril, _dot_nt(qe, ke), 0.0)
    o = _dot(qe, S.astype(BF16)) + _dot(sc.astype(BF16), vb)
    b_last = b[b.shape[0] - 1:, :]
    kd = (k * jnp.exp(b_last - b)).astype(BF16)
    dcol = _dot_tn(la_hi, ones_c) + _dot_tn(la_lo, ones_c)
    dec = jnp.exp(dcol)
    s_new = S * jnp.concatenate([dec, dec], axis=1) + _dot_tn(kd, vb)
    return o, s_new


def _gla_prompt_kernel(q_ref, k_ref, v_ref, la_ref, gn_ref, o_ref, sfin_ref, s_scr):
    j = pl.program_id(0)
    nb, tb, _ = q_ref.shape
    C = GLA_CHUNK

    @pl.when(j == 0)
    def _():
        s_scr[...] = jnp.zeros_like(s_scr)

    row = lax.broadcasted_iota(jnp.int32, (C, C), 0)
    col = lax.broadcasted_iota(jnp.int32, (C, C), 1)
    tril = col <= row
    lmat = tril.astype(BF16)
    ones_c = jnp.ones((C, LANES), BF16)
    gn = gn_ref[...]
    for b in range(nb):
        for h in range(GLA_HEADS):
            ksl = slice(h * GLA_DK_HEAD, (h + 1) * GLA_DK_HEAD)
            vsl = slice(h * GLA_DV_HEAD, (h + 1) * GLA_DV_HEAD)
            S = s_scr[b, h]
            for c in range(tb // C):
                rows = slice(c * C, (c + 1) * C)
                o, S = _gla_unit(q_ref[b, rows, ksl], k_ref[b, rows, ksl], v_ref[b, rows, vsl],
                                 la_ref[b, rows, ksl], S, tril, lmat, ones_c)
                o_ref[b, rows, vsl] = _rms_lanes(o, gn)
            s_scr[b, h] = S

    @pl.when(j == pl.num_programs(0) - 1)
    def _():
        sfin_ref[...] = s_scr[...]


def _gla_prompt_call(proj, la, gn, tb):
    B, T, _ = proj.shape
    st = (B, GLA_HEADS, GLA_DK_HEAD, GLA_DV_HEAD)
    return pl.pallas_call(
        _gla_prompt_kernel,
        grid=(T // tb,),
        in_specs=[pl.BlockSpec((B, tb, GLA_DK), lambda j: (0, j, 0)),
                  pl.BlockSpec((B, tb, GLA_DK), lambda j: (0, j, 1)),
                  pl.BlockSpec((B, tb, GLA_DV), lambda j: (0, j, 1)),
                  pl.BlockSpec((B, tb, GLA_DK), lambda j: (0, j, 0)),
                  pl.BlockSpec((1, GLA_DV_HEAD), lambda j: (0, 0))],
        out_specs=[pl.BlockSpec((B, tb, GLA_DV), lambda j: (0, j, 0)),
                   pl.BlockSpec(st, lambda j: (0, 0, 0, 0))],
        out_shape=[jax.ShapeDtypeStruct((B, T, GLA_DV), F32), jax.ShapeDtypeStruct(st, F32)],
        scratch_shapes=[pltpu.VMEM(st, F32)],
        compiler_params=_cparams("arbitrary"),
        name="gla_prompt",
    )(proj, proj, proj, la, gn)


def _gla_sample_kernel(q_ref, k_ref, v_ref, la_ref, sin_ref, gn_ref, o_ref, sout_ref):
    sb, ts, _ = q_ref.shape
    R = sb * ts
    row = lax.broadcasted_iota(jnp.int32, (R, R), 0)
    col = lax.broadcasted_iota(jnp.int32, (R, R), 1)
    shift = ts.bit_length() - 1
    same = jnp.right_shift(row, shift) == jnp.right_shift(col, shift)
    tril = same & (col <= row)
    lmat = tril.astype(BF16)
    bones = same.astype(BF16)
    ones_t = jnp.ones((ts, LANES), F32)
    gn = gn_ref[...]
    for h in range(GLA_HEADS):
        ksl = slice(h * GLA_DK_HEAD, (h + 1) * GLA_DK_HEAD)
        vsl = slice(h * GLA_DV_HEAD, (h + 1) * GLA_DV_HEAD)
        q = q_ref[:, :, ksl].reshape(R, GLA_DK_HEAD) * (GLA_DK_HEAD ** -0.5)
        k = k_ref[:, :, ksl].reshape(R, GLA_DK_HEAD)
        v = v_ref[:, :, vsl].reshape(R, GLA_DV_HEAD)
        la = la_ref[:, :, ksl].reshape(R, GLA_DK_HEAD)
        la_hi, la_lo = _split2(la)
        b = _dot(lmat, la_hi) + _dot(lmat, la_lo)
        btot = _dot(bones, la_hi) + _dot(bones, la_lo)
        qe = q * jnp.exp(b)
        ke = (k * jnp.exp(-b)).astype(BF16)
        kd = k * jnp.exp(btot - b)
        vb = v.astype(BF16)
        sc = jnp.where(tril, _dot_nt(qe.astype(BF16), ke), 0.0)
        o_intra = _dot(sc.astype(BF16), vb)
        la_hi32, la_lo32 = la_hi.astype(F32), la_lo.astype(F32)
        for i in range(sb):
            rs = slice(i * ts, (i + 1) * ts)
            S = sin_ref[i, h]
            o = _dot(qe[rs].astype(BF16), S.astype(BF16)) + o_intra[rs]
            kv = _dot_tn(kd[rs].astype(BF16), v[rs].astype(BF16))
            dec = jnp.exp(_dot_tn(la_hi32[rs].astype(BF16), ones_t.astype(BF16))
                          + _dot_tn(la_lo32[rs].astype(BF16), ones_t.astype(BF16)))
            sout_ref[i, h] = S * jnp.concatenate([dec, dec], axis=1) + kv
            o_ref[i, :, vsl] = _rms_lanes(o, gn)


def _gla_sample_call(proj, la, state, l, gn, sb):
    NS, TS, _ = proj.shape
    st_shape = (sb, GLA_HEADS, GLA_DK_HEAD, GLA_DV_HEAD)
    st_spec = pl.BlockSpec(st_shape, lambda i: (i, 0, 0, 0))
    return pl.pallas_call(
        _gla_sample_kernel,
        grid=(NS // sb,),
        in_specs=[pl.BlockSpec((sb, TS, GLA_DK), lambda i: (i, 0, 0)),
                  pl.BlockSpec((sb, TS, GLA_DK), lambda i: (i, 0, 1)),
                  pl.BlockSpec((sb, TS, GLA_DV), lambda i: (i, 0, 1)),
                  pl.BlockSpec((sb, TS, GLA_DK), lambda i: (i, 0, 0)),
                  pl.BlockSpec((None,) + st_shape, lambda i: (l, i, 0, 0, 0)),
                  pl.BlockSpec((1, GLA_DV_HEAD), lambda i: (0, 0))],
        out_specs=[pl.BlockSpec((sb, TS, GLA_DV), lambda i: (i, 0, 0)), st_spec],
        out_shape=[jax.ShapeDtypeStruct((NS, TS, GLA_DV), F32),
                   jax.ShapeDtypeStruct(state.shape[1:], F32)],
        compiler_params=_cparams("arbitrary"),
        name="gla_sample",
    )(proj, proj, proj, la, state, gn)


def _dil_prompt_kernel(q_ref, k_ref, v_ref, bias_ref, o_ref, lse_ref, kc_scr, vc_scr, *, d):
    n = pl.program_id(1)
    h = pl.program_id(2)
    nsub = SUPER // (Q_BLOCK * d)
    scale = HEAD_DIM ** -0.5

    @pl.when(n == 0)
    def _():
        kc_scr[h] = jnp.zeros(kc_scr.shape[1:], BF16)
        vc_scr[h] = jnp.zeros(vc_scr.shape[1:], BF16)

    neg_first = jnp.where(n == 0, NEG, 0.0)
    bias_p = bias_ref[:, 0:Q_BLOCK]
    bias_c = bias_ref[:, Q_BLOCK:2 * Q_BLOCK]
    for p in range(d):
        cs = slice(p * Q_BLOCK, (p + 1) * Q_BLOCK)
        kp = kc_scr[h, cs, :]
        vp = vc_scr[h, cs, :]
        for i in range(nsub):
            start = p + d * Q_BLOCK * i
            rows = pl.ds(start, Q_BLOCK, stride=d) if d > 1 else pl.ds(start, Q_BLOCK)
            q = q_ref[rows, :].astype(BF16)
            kc = k_ref[rows, :].astype(BF16)
            vc = v_ref[rows, :].astype(BF16)
            lp = _dot_nt(q, kp) * scale + bias_p
            if i == 0:
                lp = lp + neg_first
            lc = _dot_nt(q, kc) * scale + bias_c
            m = jnp.maximum(jnp.max(lp, axis=-1, keepdims=True), jnp.max(lc, axis=-1, keepdims=True))
            pp = jnp.exp(lp - m)
            pc = jnp.exp(lc - m)
            s = jnp.sum(pp, axis=-1, keepdims=True) + jnp.sum(pc, axis=-1, keepdims=True)
            acc = _dot(pp.astype(BF16), vp) + _dot(pc.astype(BF16), vc)
            o_ref[rows, :] = acc / s
            lse_ref[rows, :] = jnp.broadcast_to(m + jnp.log(s), (Q_BLOCK, LANES))
            kp, vp = kc, vc
        kc_scr[h, cs, :] = kp
        vc_scr[h, cs, :] = vp


def _dil_prompt_call(qkv, bias_g, g, d):
    B, _, T, _ = qkv.shape
    plane = lambda seg: (seg - SEG_BQ + g) * GROUP_HEADS

    def spec(seg):
        base = plane(seg)
        return pl.BlockSpec((None, None, SUPER, HEAD_DIM), lambda b, n, h: (b, base + h, n, 0))

    out_spec = pl.BlockSpec((None, None, SUPER, HEAD_DIM), lambda b, n, h: (b, h, n, 0))
    out_sds = jax.ShapeDtypeStruct((B, GROUP_HEADS, T, HEAD_DIM), F32)
    carry = pltpu.VMEM((GROUP_HEADS, d * Q_BLOCK, HEAD_DIM), BF16)
    return pl.pallas_call(
        functools.partial(_dil_prompt_kernel, d=d),
        grid=(B, T // SUPER, GROUP_HEADS),
        in_specs=[spec(SEG_BQ), spec(SEG_BK), spec(SEG_BV),
                  pl.BlockSpec((None, Q_BLOCK, 2 * Q_BLOCK), lambda b, n, h: (h, 0, 0))],
        out_specs=[out_spec, out_spec],
        out_shape=[out_sds, out_sds],
        scratch_shapes=[carry, carry],
        compiler_params=_cparams("arbitrary", "arbitrary", "arbitrary"),
        name="dilated_prompt_g%d" % g,
    )(qkv, qkv, qkv, bias_g)


def _dil_sample_kernel(qkv_ref, c0_ref, c1_ref, c2_ref, bias_ref, o_ref):
    ts = qkv_ref.shape[1]
    scale = HEAD_DIM ** -0.5
    qrows = GROUP_HEADS * ts
    n2 = c2_ref.shape[0] * c2_ref.shape[1]
    xs = (c0_ref[...].astype(BF16), c1_ref[...].astype(BF16),
          c2_ref[...].reshape(n2, LANES).astype(BF16))
    zpad = jnp.zeros((LANES - qrows, HEAD_DIM), F32)

    def planes(seg):
        return jnp.concatenate([qkv_ref[seg * GROUP_HEADS + h] for h in range(GROUP_HEADS)], axis=0)

    qs = [planes(g).astype(BF16) for g in range(N_GROUPS)]
    parts = []
    off = 0
    for g in range(N_GROUPS):
        w = xs[g].shape[0]
        parts.append((_dot_nt(qs[g], xs[g]) * scale + bias_ref[:, off:off + w], xs[g], True))
        off += w
    for g in range(N_GROUPS):
        kn = jnp.concatenate([planes(N_GROUPS + g), zpad], axis=0).astype(BF16)
        vn = jnp.concatenate([planes(2 * N_GROUPS + g), zpad], axis=0).astype(BF16)
        parts.append((_dot_nt(qs[g], kn) * scale + bias_ref[:, off:off + LANES], vn, False))
        off += LANES
    m = parts[0][0].max(axis=-1, keepdims=True)
    for lg, _, _ in parts[1:]:
        m = jnp.maximum(m, lg.max(axis=-1, keepdims=True))
    s = jnp.zeros((qrows, 1), F32)
    acc = jnp.zeros((qrows, HEAD_DIM), F32)
    for lg, vv, cached in parts:
        p = jnp.exp(lg - m)
        s = s + jnp.sum(p, axis=-1, keepdims=True)
        if cached:
            p = pltpu.roll(p, GROUP_HEADS, axis=1)
        acc = acc + _dot(p.astype(BF16), vv)
    res = acc / s
    for h in range(GROUP_HEADS):
        o_ref[:, h * HEAD_DIM:(h + 1) * HEAD_DIM] = res[h * ts:(h + 1) * ts]


def _dil_sample_call(qkv, caches, l, bias_s, ns, ts):
    c0, c1, c2 = caches
    return pl.pallas_call(
        _dil_sample_kernel,
        grid=(ns,),
        in_specs=[pl.BlockSpec((None, N_QKV, ts, HEAD_DIM), lambda i: (0, 0, i, 0)),
                  pl.BlockSpec((None, None) + c0.shape[2:], lambda i: (l, i, 0, 0)),
                  pl.BlockSpec((None, None) + c1.shape[2:], lambda i: (l, i, 0, 0)),
                  pl.BlockSpec((None, None, c2.shape[2], ts * CACHE_ROW, LANES),
                               lambda i: (l, i, 0, 0, 0)),
                  pl.BlockSpec(bias_s.shape, lambda i: (0, 0))],
        out_specs=pl.BlockSpec((None, ts, B_OUT), lambda i: (i, 0, 0)),
        out_shape=jax.ShapeDtypeStruct((ns, ts, B_OUT), F32),
        compiler_params=_cparams("arbitrary"),
        name="dilated_sample",
    )(qkv, c0, c1, c2, bias_s)


def _merge_kernel(*refs, combine):
    if combine:
        (x_ref, mod_ref, oa_ref, gr_ref, ga_ref, gb_ref, o0_ref, o1_ref, o2_ref,
         l0_ref, l1_ref, l2_ref, wpa_ref, wpb_ref, wo_ref, out_ref) = refs
    else:
        (x_ref, mod_ref, oa_ref, gr_ref, ga_ref, gb_ref, ob_ref,
         wpa_ref, wpb_ref, wo_ref, out_ref) = refs
    gb, rb, d = x_ref.shape
    tm = gb * rb
    if combine:
        o_refs = (o0_ref, o1_ref, o2_ref)
        l_refs = (l0_ref, l1_ref, l2_ref)
        parts = []
        for h in range(GROUP_HEADS):
            ls = [l[h] for l in l_refs]
            m = jnp.maximum(jnp.maximum(ls[0], ls[1]), ls[2])
            es = [jnp.exp(l - m) for l in ls]
            den = es[0] + es[1] + es[2]
            parts.append(sum((es[g] / den) * o_refs[g][h] for g in range(N_GROUPS)))
        ob = jnp.concatenate(parts, axis=1)
    else:
        ob = ob_ref[...].reshape(tm, B_OUT)
    gr = gr_ref[...].reshape(tm, GLA_DV)
    oa = oa_ref[...].reshape(tm, GLA_DV) * (gr * _sigmoid(gr))
    m1 = _dot(oa.astype(BF16), wpa_ref[...])
    m2 = _dot(ob.astype(BF16), wpb_ref[...])
    merged = (_sigmoid(ga_ref[...].reshape(tm, d)) * m1
              + _sigmoid(gb_ref[...].reshape(tm, d)) * m2)
    y = _dot(merged.astype(BF16), wo_ref[...]).reshape(gb, rb, d)
    out_ref[...] = x_ref[...] + mod_ref[...][:, 2:3, :] * y


def _merge_call(x, mod, oa, proj, ob_parts, wpa, wpb, wo, gb, rb):
    G, R, D = x.shape
    combine = len(ob_parts) > 1
    tm = gb * rb
    tok = lambda w, c: pl.BlockSpec((gb, rb, w), lambda a, b: (a, b, c))
    const = lambda shp: pl.BlockSpec(shp, lambda a, b: (0, 0))
    in_specs = [tok(D, 0), pl.BlockSpec((gb, N_MOD, D), lambda a, b: (a, 0, 0)),
                tok(GLA_DV, 0), tok(GLA_DV, 2), tok(D, 3), tok(D, 4)]
    args = [x, mod, oa, proj, proj, proj]
    if combine:
        os_, ls_ = ob_parts
        head_major = pl.BlockSpec((None, GROUP_HEADS, tm, HEAD_DIM), lambda a, b: (a, 0, b, 0))
        in_specs += [head_major] * (2 * N_GROUPS)
        args += list(os_) + list(ls_)
    else:
        in_specs += [tok(B_OUT, 0)]
        args += list(ob_parts)
    in_specs += [const((GLA_DV, D)), const((B_OUT, D)), const((D, D))]
    args += [wpa, wpb, wo]
    return pl.pallas_call(
        functools.partial(_merge_kernel, combine=combine),
        grid=(G // gb, R // rb),
        in_specs=in_specs,
        out_specs=tok(D, 0),
        out_shape=jax.ShapeDtypeStruct((G, R, D), F32),
        compiler_params=_cparams("arbitrary", "arbitrary"),
        name="merge_combine" if combine else "merge",
    )(*args)


def _mlp_kernel(x_ref, mod_ref, g_ref, wu_ref, wd_ref, out_ref, u_scr, acc_scr):
    f = pl.program_id(2)
    gb, rb, d = x_ref.shape
    tm = gb * rb

    @pl.when(f == 0)
    def _():
        mod = mod_ref[...]
        u = _norm_mod(x_ref[...], g_ref[...], mod[:, 3:4, :], mod[:, 4:5, :])
        u_scr[...] = u.reshape(tm, d).astype(BF16)
        acc_scr[...] = jnp.zeros_like(acc_scr)

    hid = jnp.maximum(_dot(u_scr[...], wu_ref[...]), 0.0)
    acc_scr[...] += _dot((hid * hid).astype(BF16), wd_ref[...])

    @pl.when(f == pl.num_programs(2) - 1)
    def _():
        out_ref[...] = x_ref[...] + mod_ref[...][:, 5:6, :] * acc_scr[...].reshape(gb, rb, d)


def _mlp_call(x, mod, g, wu, wd, gb, rb, tf):
    G, R, D = x.shape
    F = wu.shape[1]
    tm = gb * rb
    return pl.pallas_call(
        _mlp_kernel,
        grid=(G // gb, R // rb, F // tf),
        in_specs=[pl.BlockSpec((gb, rb, D), lambda a, b, f: (a, b, 0)),
                  pl.BlockSpec((gb, N_MOD, D), lambda a, b, f: (a, 0, 0)),
                  pl.BlockSpec((1, D), lambda a, b, f: (0, 0)),
                  pl.BlockSpec((D, tf), lambda a, b, f: (0, f)),
                  pl.BlockSpec((tf, D), lambda a, b, f: (f, 0))],
        out_specs=pl.BlockSpec((gb, rb, D), lambda a, b, f: (a, b, 0)),
        out_shape=jax.ShapeDtypeStruct((G, R, D), F32),
        scratch_shapes=[pltpu.VMEM((tm, D), BF16), pltpu.VMEM((tm, D), F32)],
        compiler_params=_cparams("arbitrary", "arbitrary", "arbitrary"),
        name="mlp",
    )(x, mod, g, wu, wd)


def _t5_bucket(dist):
    dist = np.asarray(dist)
    large = REL_MAX_EXACT + (np.log(np.maximum(dist, 1) / REL_MAX_EXACT)
                             / np.log(REL_MAX_DIST / REL_MAX_EXACT)
                             * (REL_BUCKETS - REL_MAX_EXACT)).astype(np.int32)
    large = np.minimum(large, REL_BUCKETS - 1)
    return np.where(dist < REL_MAX_EXACT, dist, large).astype(np.int32)


def _tap_bias(rel_bias, g):
    d = DIL_PAIRS[g][1]
    idx = _t5_bucket(d * np.arange(TAPS))
    return rel_bias[idx][:, g * GROUP_HEADS:(g + 1) * GROUP_HEADS].astype(F32)


def _prompt_bias(rel_bias, g):
    tb = _tap_bias(rel_bias, g)
    w = jnp.concatenate([tb[::-1].T, jnp.full((GROUP_HEADS, Q_BLOCK), NEG, F32)], axis=1)
    flat = jnp.tile(w, (1, Q_BLOCK))[:, :Q_BLOCK * 2 * Q_BLOCK]
    return flat.reshape(GROUP_HEADS, Q_BLOCK, 2 * Q_BLOCK)


def _dist_bias(tb, d, n):
    h = tb.shape[1]
    v = jnp.concatenate([tb[:, None, :], jnp.full((TAPS, d - 1, h), NEG, F32)], axis=1)
    v = v.reshape(TAPS * d, h)
    if n > TAPS * d:
        v = jnp.concatenate([v, jnp.full((n - TAPS * d, h), NEG, F32)], axis=0)
    return v[:n].T


def _sample_bias(rel_bias, ts, widths):
    H = GROUP_HEADS
    heads = np.arange(H)
    own = jnp.asarray(np.arange(CACHE_ROW)[None, :] == heads[:, None])
    same = jnp.asarray(heads[:, None] == heads[None, :])
    cache_parts, new_parts = [], []
    for g, (w, d) in enumerate(DIL_PAIRS):
        w_eff = widths[g]
        bv = _dist_bias(_tap_bias(rel_bias, g), d, w_eff + ts)
        rows = jnp.stack([bv[:, t + 1:t + 1 + w_eff][:, ::-1] for t in range(ts)], axis=1)
        if g == N_GROUPS - 1:
            rows = rows.reshape(H, ts, w_eff // d, d)[..., :ts].reshape(H, ts, (w_eff // d) * ts)
        tab = jnp.where(own[:, None, None, :], rows[..., None], NEG)
        cache_parts.append(tab.reshape(H * ts, -1))
        tri = jnp.stack([jnp.concatenate([bv[:, :t + 1][:, ::-1],
                                          jnp.full((H, ts - 1 - t), NEG, F32)], axis=1)
                         for t in range(ts)], axis=1)
        blk = jnp.where(same[:, None, :, None], tri[:, :, None, :], NEG).reshape(H * ts, H * ts)
        new_parts.append(jnp.concatenate([blk, jnp.full((H * ts, LANES - H * ts), NEG, F32)], axis=1))
    return jnp.concatenate(cache_parts + new_parts, axis=1)


def _window_rows(qkv, g, lo, lead):
    k0 = (SEG_BK - SEG_BQ + g) * GROUP_HEADS
    v0 = (SEG_BV - SEG_BQ + g) * GROUP_HEADS
    kv = jnp.stack([qkv[:, k0:k0 + GROUP_HEADS, lo:], qkv[:, v0:v0 + GROUP_HEADS, lo:]], axis=1)
    kv = jnp.transpose(kv, (0, 3, 1, 2, 4))
    return kv.reshape(lead + kv.shape[2:])


def kernel(x_prompt, x_sample, c_prompt, c_sample, state_gla, cache_win1, cache_win2, cache_win3,
           rel_bias, norm1_g, norm2_g, w_mod, b_mod, w_in, w_alpha, b_alpha, gla_norm_g,
           qn_g, kn_g, w_pa, w_pb, w_o, w_up, w_down):
    B, T, D = x_prompt.shape
    NS, TS, _ = x_sample.shape
    L = w_mod.shape[0]
    widths = [c.shape[2] for c in (cache_win1, cache_win2, cache_win3)]
    d2 = DIL_PAIRS[2][1]
    assert D == D_MODEL and TS == 8 and T % SUPER == 0
    assert widths[2] % d2 == 0 and TS <= d2
    caches = (cache_win1.astype(F32).reshape(L, NS, widths[0] * CACHE_ROW, LANES),
              cache_win2.astype(F32).reshape(L, NS, widths[1] * CACHE_ROW, LANES),
              cache_win3.astype(F32).reshape(L, NS, widths[2] // d2, d2 * CACHE_ROW, LANES))
    state = state_gla.astype(F32)

    mods = _mod_call(jnp.concatenate([c_prompt, c_sample], axis=0).astype(F32), w_mod, b_mod)
    mods = mods.reshape(L, B + NS, N_MOD, D)

    sizes = (GLA_DK, GLA_DK, GLA_DV, GLA_DV, GLA_RANK, B_WIDTH, B_WIDTH, B_WIDTH, D_MODEL, D_MODEL)
    offs = np.concatenate([[0], np.cumsum(sizes)])
    piece = lambda i: w_in[:, :, offs[i]:offs[i + 1]]
    w_main = jnp.concatenate([piece(i) for i in (0, 1, 2, 3, 8, 9, 5, 6, 7)], axis=-1).astype(BF16)
    w_glr = jnp.pad(piece(4), ((0, 0), (0, 0), (0, LANES - GLA_RANK))).astype(BF16)
    w_al = jnp.pad(w_alpha, ((0, 0), (0, LANES - GLA_RANK), (0, 0))).astype(BF16)
    w_pa_b, w_pb_b, w_o_b = w_pa.astype(BF16), w_pb.astype(BF16), w_o.astype(BF16)
    w_up_b, w_down_b = w_up.astype(BF16), w_down.astype(BF16)

    bias_p = [_prompt_bias(rel_bias, g) for g in range(N_GROUPS)]
    bias_s = _sample_bias(rel_bias, TS, widths)

    tm = 1024
    xp = x_prompt.astype(F32)
    xs = x_sample.astype(F32)
    gla_p, gla_s = [], []
    win_p = [[] for _ in range(N_GROUPS)]
    win_s = [[] for _ in range(N_GROUPS)]
    row2 = lambda v: v.reshape(1, -1)
    for l in range(L):
        mod_p, mod_s = mods[l, :B], mods[l, B:]
        common = (row2(norm1_g[l]), w_main[l], w_glr[l], w_al[l], row2(b_alpha[l]),
                  row2(qn_g[l]), row2(kn_g[l]))
        gn = row2(gla_norm_g[l])

        proj, la, qkv = _inproj_call(xp, mod_p, *common, gb=1, rb=tm)
        oa, s_fin = _gla_prompt_call(proj, la, gn, tb=256)
        outs = [_dil_prompt_call(qkv, bias_p[g], g, DIL_PAIRS[g][1]) for g in range(N_GROUPS)]
        xp = _merge_call(xp, mod_p, oa, proj, ([o for o, _ in outs], [s for _, s in outs]),
                         w_pa_b[l], w_pb_b[l], w_o_b[l], gb=1, rb=tm // 2)
        xp = _mlp_call(xp, mod_p, row2(norm2_g[l]), w_up_b[l], w_down_b[l], gb=1, rb=tm, tf=1024)
        gla_p.append(s_fin)
        for g, (w, d) in enumerate(DIL_PAIRS):
            keep = min(w, T)
            win_p[g].append(_window_rows(qkv, g, T - keep, (B, keep)))

        proj, la, qkv = _inproj_call(xs, mod_s, *common, gb=NS, rb=TS)
        oa, s_new = _gla_sample_call(proj, la, state, l, gn, sb=8)
        ob = _dil_sample_call(qkv, caches, l, bias_s, NS, TS)
        xs = _merge_call(xs, mod_s, oa, proj, (ob,), w_pa_b[l], w_pb_b[l], w_o_b[l], gb=NS, rb=TS)
        xs = _mlp_call(xs, mod_s, row2(norm2_g[l]), w_up_b[l], w_down_b[l], gb=NS, rb=TS, tf=1024)
        gla_s.append(s_new)
        for g in range(N_GROUPS):
            win_s[g].append(_window_rows(qkv, g, 0, (NS, TS)))

    return (xp.astype(x_prompt.dtype), xs.astype(x_sample.dtype), jnp.stack(gla_p), jnp.stack(gla_s),
            jnp.stack(win_p[0]), jnp.stack(win_s[0]), jnp.stack(win_p[1]), jnp.stack(win_s[1]),
            jnp.stack(win_p[2]), jnp.stack(win_s[2]))
```

```python
import functools

import numpy as np
import jax
import jax.numpy as jnp
from jax import lax
from jax.experimental import pallas as pl
from jax.experimental.pallas import tpu as pltpu

F32 = jnp.float32
BF16 = jnp.bfloat16

D_MODEL = 1024
GLA_HEADS = 4
GLA_DK_HEAD = 128
GLA_DV_HEAD = 256
GLA_DK = GLA_HEADS * GLA_DK_HEAD
GLA_DV = GLA_HEADS * GLA_DV_HEAD
GLA_RANK = 16
GLA_GATE_NORM = 16.0
GLA_CHUNK = 64
DIL_PAIRS = ((128, 1), (512, 4), (2048, 16))
N_GROUPS = 3
GROUP_HEADS = 4
HEAD_DIM = 128
B_WIDTH = N_GROUPS * GROUP_HEADS * HEAD_DIM
B_OUT = GROUP_HEADS * HEAD_DIM
TAPS = 129
Q_BLOCK = 128
REL_BUCKETS = 32
REL_MAX_EXACT = 16
REL_MAX_DIST = 2048
D_FF = 4 * D_MODEL
N_MOD = 6
EPS = 1e-6
NEG = -1e30

LANES = 128
CACHE_ROW = 2 * GROUP_HEADS

SEG = 512
SEG_MAIN = 10
NSEG = 19
MAIN_W = SEG_MAIN * SEG
SEG_BQ, SEG_BK, SEG_BV = 10, 13, 16
N_QKV = (NSEG - SEG_MAIN) * GROUP_HEADS
SUPER = 2048

VMEM_LIMIT = 48 * 1024 * 1024


def _cparams(*sem):
    return pltpu.CompilerParams(dimension_semantics=sem, vmem_limit_bytes=VMEM_LIMIT)


def _dot(a, b):
    return jnp.dot(a, b, preferred_element_type=F32)


def _dot_nt(a, b):
    return lax.dot_general(a, b, (((1,), (1,)), ((), ())), preferred_element_type=F32)


def _dot_tn(a, b):
    return lax.dot_general(a, b, (((0,), (0,)), ((), ())), preferred_element_type=F32)


def _sigmoid(x):
    return 1.0 / (1.0 + jnp.exp(-x))


def _norm_mod(x, g, shift, scale):
    ms = jnp.mean(x * x, axis=-1, keepdims=True)
    n = x * lax.rsqrt(ms + EPS) * g
    return n * (1.0 + scale) + shift


def _rms_lanes(x, g):
    ms = jnp.mean(x * x, axis=-1, keepdims=True)
    return x * lax.rsqrt(ms + EPS) * g


def _split2(x):
    hi = x.astype(BF16)
    lo = (x - hi.astype(F32)).astype(BF16)
    return hi, lo


def _mod_kernel(c_ref, w_ref, b_ref, o_ref):
    c = c_ref[...]
    a = (c * _sigmoid(c)).astype(BF16)
    o_ref[...] = _dot(a, w_ref[...].astype(BF16)) + b_ref[...]


def _mod_call(c_all, w_mod, b_mod):
    L, D, N = w_mod.shape
    nb = c_all.shape[0]
    tn = 1024
    return pl.pallas_call(
        _mod_kernel,
        grid=(L, N // tn),
        in_specs=[pl.BlockSpec((nb, D), lambda l, n: (0, 0)),
                  pl.BlockSpec((None, D, tn), lambda l, n: (l, 0, n)),
                  pl.BlockSpec((None, 1, tn), lambda l, n: (l, 0, n))],
        out_specs=pl.BlockSpec((None, nb, tn), lambda l, n: (l, 0, n)),
        out_shape=jax.ShapeDtypeStruct((L, nb, N), F32),
        compiler_params=_cparams("arbitrary", "arbitrary"),
        name="adaln_mod",
    )(c_all, w_mod, b_mod.reshape(L, 1, N))


def _prenorm_kernel(x_ref, mod_ref, g_ref, wglr_ref, wal_ref, bal_ref, u_ref, la_ref):
    gb, rb, d = x_ref.shape
    tm = gb * rb
    mod = mod_ref[...]
    u = _norm_mod(x_ref[...], g_ref[...], mod[:, 0:1, :], mod[:, 1:2, :])
    ub = u.reshape(tm, d).astype(BF16)
    u_ref[...] = ub
    glr = _dot(ub, wglr_ref[...])
    z = _dot(glr.astype(BF16), wal_ref[...]) + bal_ref[...]
    la = (jnp.minimum(z, 0.0) - jnp.log1p(jnp.exp(-jnp.abs(z)))) * (1.0 / GLA_GATE_NORM)
    la_ref[...] = la.reshape(gb, rb, GLA_DK)


def _prenorm_call(x, mod, g, wglr, wal, bal, gb, rb):
    G, R, D = x.shape
    tm = gb * rb
    na, nr = G // gb, R // rb
    const2 = lambda a, b: (0, 0)
    return pl.pallas_call(
        _prenorm_kernel,
        grid=(na, nr),
        in_specs=[pl.BlockSpec((gb, rb, D), lambda a, b: (a, b, 0)),
                  pl.BlockSpec((gb, N_MOD, D), lambda a, b: (a, 0, 0)),
                  pl.BlockSpec((1, D), const2),
                  pl.BlockSpec((D, LANES), const2),
                  pl.BlockSpec((LANES, GLA_DK), const2),
                  pl.BlockSpec((1, GLA_DK), const2)],
        out_specs=[pl.BlockSpec((None, tm, D), lambda a, b: (a, b, 0)),
                   pl.BlockSpec((gb, rb, GLA_DK), lambda a, b: (a, b, 0))],
        out_shape=[jax.ShapeDtypeStruct((na, nr * tm, D), BF16),
                   jax.ShapeDtypeStruct((G, R, GLA_DK), F32)],
        compiler_params=_cparams("arbitrary", "arbitrary"),
        name="prenorm",
    )(x, mod, g, wglr, wal, bal)


def _inproj_kernel(u_ref, w_ref, qn_ref, kn_ref, proj_ref, qkv_ref):
    s = pl.program_id(2)
    tm = u_ref.shape[0]
    rc = min(tm, 512)

    @pl.when(s < SEG_MAIN)
    def _():
        proj_ref[...] = _dot(u_ref[...], w_ref[...])

    def to_heads(r0, rows, val):
        for h in range(GROUP_HEADS):
            qkv_ref[h, r0:r0 + rows, :] = val[:, h * HEAD_DIM:(h + 1) * HEAD_DIM]

    def normed(gain_ref):
        g = gain_ref[...]
        for c in range(tm // rc):
            r = _dot(u_ref[c * rc:(c + 1) * rc, :], w_ref[...])
            to_heads(c * rc, rc, jnp.concatenate(
                [_rms_lanes(r[:, h * HEAD_DIM:(h + 1) * HEAD_DIM], g) for h in range(GROUP_HEADS)],
                axis=1))

    @pl.when((s >= SEG_BQ) & (s < SEG_BK))
    def _():
        normed(qn_ref)

    @pl.when((s >= SEG_BK) & (s < SEG_BV))
    def _():
        normed(kn_ref)

    @pl.when(s >= SEG_BV)
    def _():
        to_heads(0, tm, _dot(u_ref[...], w_ref[...]))


def _inproj_call(u, w, qn, kn, tm):
    A, N, D = u.shape
    const2 = lambda a, b, s: (0, 0)
    return pl.pallas_call(
        _inproj_kernel,
        grid=(A, N // tm, NSEG),
        in_specs=[pl.BlockSpec((None, tm, D), lambda a, b, s: (a, b, 0)),
                  pl.BlockSpec((D, SEG), lambda a, b, s: (0, s)),
                  pl.BlockSpec((1, HEAD_DIM), const2),
                  pl.BlockSpec((1, HEAD_DIM), const2)],
        out_specs=[pl.BlockSpec((None, tm, SEG), lambda a, b, s: (a, b, jnp.minimum(s, SEG_MAIN - 1))),
                   pl.BlockSpec((None, GROUP_HEADS, tm, HEAD_DIM),
                                lambda a, b, s: (a, jnp.clip(s - SEG_BQ, 0, NSEG - SEG_BQ - 1), b, 0))],
        out_shape=[jax.ShapeDtypeStruct((A, N, MAIN_W), F32),
                   jax.ShapeDtypeStruct((A, N_QKV, N, HEAD_DIM), F32)],
        compiler_params=_cparams("arbitrary", "arbitrary", "arbitrary"),
        name="in_proj",
    )(u, w, qn, kn)


def _gla_unit(q, k, v, la, S, tril, lmat, ones_c):
    la_hi, la_lo = _split2(la)
    b = _dot(lmat, la_hi) + _dot(lmat, la_lo)
    qe = (q * (GLA_DK_HEAD ** -0.5) * jnp.exp(b)).astype(BF16)
    ke = (k * jnp.exp(-b)).astype(BF16)
    vb = v.astype(BF16)
    sc = jnp.where(tril, _dot_nt(qe, ke), 0.0)
    o = _dot(qe, S.astype(BF16)) + _dot(sc.astype(BF16), vb)
    b_last = b[b.shape[0] - 1:, :]
    kd = (k * jnp.exp(b_last - b)).astype(BF16)
    dcol = _dot_tn(la_hi, ones_c) + _dot_tn(la_lo, ones_c)
    dec = jnp.exp(dcol)
    s_new = S * jnp.concatenate([dec, dec], axis=1) + _dot_tn(kd, vb)
    return o, s_new


def _gla_prompt_kernel(q_ref, k_ref, v_ref, la_ref, gn_ref, o_ref, sfin_ref, s_scr):
    j = pl.program_id(0)
    nb, tb, _ = q_ref.shape
    C = GLA_CHUNK

    @pl.when(j == 0)
    def _():
        s_scr[...] = jnp.zeros_like(s_scr)

    row = lax.broadcasted_iota(jnp.int32, (C, C), 0)
    col = lax.broadcasted_iota(jnp.int32, (C, C), 1)
    tril = col <= row
    lmat = tril.astype(BF16)
    ones_c = jnp.ones((C, LANES), BF16)
    gn = gn_ref[...]
    for b in range(nb):
        for h in range(GLA_HEADS):
            ksl = slice(h * GLA_DK_HEAD, (h + 1) * GLA_DK_HEAD)
            vsl = slice(h * GLA_DV_HEAD, (h + 1) * GLA_DV_HEAD)
            S = s_scr[b, h]
            for c in range(tb // C):
                rows = slice(c * C, (c + 1) * C)
                o, S = _gla_unit(q_ref[b, rows, ksl], k_ref[b, rows, ksl], v_ref[b, rows, vsl],
                                 la_ref[b, rows, ksl], S, tril, lmat, ones_c)
                o_ref[b, rows, vsl] = _rms_lanes(o, gn)
            s_scr[b, h] = S

    @pl.when(j == pl.num_programs(0) - 1)
    def _():
        sfin_ref[...] = s_scr[...]


def _gla_prompt_call(proj, la, gn, tb):
    B, T, _ = proj.shape
    st = (B, GLA_HEADS, GLA_DK_HEAD, GLA_DV_HEAD)
    return pl.pallas_call(
        _gla_prompt_kernel,
        grid=(T // tb,),
        in_specs=[pl.BlockSpec((B, tb, GLA_DK), lambda j: (0, j, 0)),
                  pl.BlockSpec((B, tb, GLA_DK), lambda j: (0, j, 1)),
                  pl.BlockSpec((B, tb, GLA_DV), lambda j: (0, j, 1)),
                  pl.BlockSpec((B, tb, GLA_DK), lambda j: (0, j, 0)),
                  pl.BlockSpec((1, GLA_DV_HEAD), lambda j: (0, 0))],
        out_specs=[pl.BlockSpec((B, tb, GLA_DV), lambda j: (0, j, 0)),
                   pl.BlockSpec(st, lambda j: (0, 0, 0, 0))],
        out_shape=[jax.ShapeDtypeStruct((B, T, GLA_DV), F32), jax.ShapeDtypeStruct(st, F32)],
        scratch_shapes=[pltpu.VMEM(st, F32)],
        compiler_params=_cparams("arbitrary"),
        name="gla_prompt",
    )(proj, proj, proj, la, gn)


def _gla_sample_kernel(q_ref, k_ref, v_ref, la_ref, sin_ref, gn_ref, *rest):
    o_ref, sout_ref = rest[-2:]
    sb, ts, _ = q_ref.shape
    R = sb * ts
    row = lax.broadcasted_iota(jnp.int32, (R, R), 0)
    col = lax.broadcasted_iota(jnp.int32, (R, R), 1)
    shift = ts.bit_length() - 1
    same = jnp.right_shift(row, shift) == jnp.right_shift(col, shift)
    tril = same & (col <= row)
    lmat = tril.astype(BF16)
    bones = same.astype(BF16)
    ones_t = jnp.ones((ts, LANES), F32)
    gn = gn_ref[...]
    for h in range(GLA_HEADS):
        ksl = slice(h * GLA_DK_HEAD, (h + 1) * GLA_DK_HEAD)
        vsl = slice(h * GLA_DV_HEAD, (h + 1) * GLA_DV_HEAD)
        q = q_ref[:, :, ksl].reshape(R, GLA_DK_HEAD) * (GLA_DK_HEAD ** -0.5)
        k = k_ref[:, :, ksl].reshape(R, GLA_DK_HEAD)
        v = v_ref[:, :, vsl].reshape(R, GLA_DV_HEAD)
        la = la_ref[:, :, ksl].reshape(R, GLA_DK_HEAD)
        la_hi, la_lo = _split2(la)
        b = _dot(lmat, la_hi) + _dot(lmat, la_lo)
        btot = _dot(bones, la_hi) + _dot(bones, la_lo)
        qe = q * jnp.exp(b)
        ke = (k * jnp.exp(-b)).astype(BF16)
        kd = k * jnp.exp(btot - b)
        vb = v.astype(BF16)
        sc = jnp.where(tril, _dot_nt(qe.astype(BF16), ke), 0.0)
        o_intra = _dot(sc.astype(BF16), vb)
        la_hi32, la_lo32 = la_hi.astype(F32), la_lo.astype(F32)
        for i in range(sb):
            rs = slice(i * ts, (i + 1) * ts)
            S = sin_ref[i, h]
            o = _dot(qe[rs].astype(BF16), S.astype(BF16)) + o_intra[rs]
            kv = _dot_tn(kd[rs].astype(BF16), v[rs].astype(BF16))
            dec = jnp.exp(_dot_tn(la_hi32[rs].astype(BF16), ones_t.astype(BF16))
                          + _dot_tn(la_lo32[rs].astype(BF16), ones_t.astype(BF16)))
            sout_ref[i, h] = S * jnp.concatenate([dec, dec], axis=1) + kv
            o_ref[i, :, vsl] = _rms_lanes(o, gn)


def _gla_sample_call(proj, la, state, l, gn, sb, acc):
    NS, TS, _ = proj.shape
    st_shape = (None, sb, GLA_HEADS, GLA_DK_HEAD, GLA_DV_HEAD)
    st_spec = pl.BlockSpec(st_shape, lambda i: (l, i, 0, 0, 0))
    in_specs = [pl.BlockSpec((sb, TS, GLA_DK), lambda i: (i, 0, 0)),
                pl.BlockSpec((sb, TS, GLA_DK), lambda i: (i, 0, 1)),
                pl.BlockSpec((sb, TS, GLA_DV), lambda i: (i, 0, 1)),
                pl.BlockSpec((sb, TS, GLA_DK), lambda i: (i, 0, 0)),
                st_spec,
                pl.BlockSpec((1, GLA_DV_HEAD), lambda i: (0, 0))]
    args = [proj, proj, proj, la, state, gn]
    aliases = {}
    if acc is not None:
        in_specs.append(pl.BlockSpec(memory_space=pl.ANY))
        args.append(acc)
        aliases = {len(args) - 1: 1}
    return pl.pallas_call(
        _gla_sample_kernel,
        grid=(NS // sb,),
        in_specs=in_specs,
        out_specs=[pl.BlockSpec((sb, TS, GLA_DV), lambda i: (i, 0, 0)), st_spec],
        out_shape=[jax.ShapeDtypeStruct((NS, TS, GLA_DV), F32),
                   jax.ShapeDtypeStruct(state.shape, F32)],
        input_output_aliases=aliases,
        compiler_params=_cparams("arbitrary"),
        name="gla_sample",
    )(*args)


def _dil_prompt_kernel(q_ref, k_ref, v_ref, bias_ref, o_ref, lse_ref, kc_scr, vc_scr, *, d):
    n = pl.program_id(1)
    h = pl.program_id(2)
    nsub = SUPER // (Q_BLOCK * d)
    scale = HEAD_DIM ** -0.5

    @pl.when(n == 0)
    def _():
        kc_scr[h] = jnp.zeros(kc_scr.shape[1:], BF16)
        vc_scr[h] = jnp.zeros(vc_scr.shape[1:], BF16)

    bias = bias_ref[...]
    col = lax.broadcasted_iota(jnp.int32, bias.shape, 1)
    bias_first = bias + jnp.where((col < Q_BLOCK) & (n == 0), NEG, 0.0)
    ones = jnp.ones((Q_BLOCK, LANES), BF16)
    for p in range(d):
        cs = slice(p * Q_BLOCK, (p + 1) * Q_BLOCK)
        kp = kc_scr[h, cs, :]
        vp = vc_scr[h, cs, :]
        for i in range(nsub):
            start = p + d * Q_BLOCK * i
            rows = pl.ds(start, Q_BLOCK, stride=d) if d > 1 else pl.ds(start, Q_BLOCK)
            q = q_ref[rows, :].astype(BF16)
            kc = k_ref[rows, :].astype(BF16)
            vc = v_ref[rows, :].astype(BF16)
            lg = (_dot_nt(q, jnp.concatenate([kp, kc], axis=0)) * scale
                  + (bias_first if i == 0 else bias))
            m = jnp.max(lg, axis=-1, keepdims=True)
            pr = jnp.exp(lg - m).astype(BF16)
            v2 = jnp.concatenate([jnp.concatenate([vp, ones], axis=1),
                                  jnp.concatenate([vc, ones], axis=1)], axis=0)
            acc = _dot(pr, v2)
            s = acc[:, HEAD_DIM:]
            o_ref[rows, :] = acc[:, :HEAD_DIM] / s
            lse_ref[rows, :] = m + jnp.log(s)
            kp, vp = kc, vc
        kc_scr[h, cs, :] = kp
        vc_scr[h, cs, :] = vp


def _dil_prompt_call(qkv, bias_g, g, d):
    B, _, T, _ = qkv.shape
    plane = lambda seg: (seg - SEG_BQ + g) * GROUP_HEADS

    def spec(seg):
        base = plane(seg)
        return pl.BlockSpec((None, None, SUPER, HEAD_DIM), lambda b, n, h: (b, base + h, n, 0))

    out_spec = pl.BlockSpec((None, None, SUPER, HEAD_DIM), lambda b, n, h: (b, h, n, 0))
    out_sds = jax.ShapeDtypeStruct((B, GROUP_HEADS, T, HEAD_DIM), F32)
    carry = pltpu.VMEM((GROUP_HEADS, d * Q_BLOCK, HEAD_DIM), BF16)
    return pl.pallas_call(
        functools.partial(_dil_prompt_kernel, d=d),
        grid=(B, T // SUPER, GROUP_HEADS),
        in_specs=[spec(SEG_BQ), spec(SEG_BK), spec(SEG_BV),
                  pl.BlockSpec((None, Q_BLOCK, 2 * Q_BLOCK), lambda b, n, h: (h, 0, 0))],
        out_specs=[out_spec, out_spec],
        out_shape=[out_sds, out_sds],
        scratch_shapes=[carry, carry],
        compiler_params=_cparams("arbitrary", "arbitrary", "arbitrary"),
        name="dilated_prompt_g%d" % g,
    )(qkv, qkv, qkv, bias_g)


def _dil_sample_kernel(qkv_ref, c0_ref, c1_ref, c2_ref, bias_ref, o_ref):
    sb, ts, _ = o_ref.shape
    scale = HEAD_DIM ** -0.5
    qrows = GROUP_HEADS * ts
    n2 = c2_ref.shape[1] * c2_ref.shape[2]
    zpad = jnp.zeros((LANES - qrows, HEAD_DIM), F32)
    for j in range(sb):
        xs = (c0_ref[j].astype(BF16), c1_ref[j].astype(BF16),
              c2_ref[j].reshape(n2, LANES).astype(BF16))

        def planes(seg):
            return jnp.concatenate([qkv_ref[seg * GROUP_HEADS + h, j * ts:(j + 1) * ts, :]
                                    for h in range(GROUP_HEADS)], axis=0)

        qs = [planes(g).astype(BF16) for g in range(N_GROUPS)]
        parts = []
        off = 0
        for g in range(N_GROUPS):
            w = xs[g].shape[0]
            parts.append((_dot_nt(qs[g], xs[g]) * scale + bias_ref[:, off:off + w], xs[g], True))
            off += w
        for g in range(N_GROUPS):
            kn = jnp.concatenate([planes(N_GROUPS + g), zpad], axis=0).astype(BF16)
            vn = jnp.concatenate([planes(2 * N_GROUPS + g), zpad], axis=0).astype(BF16)
            parts.append((_dot_nt(qs[g], kn) * scale + bias_ref[:, off:off + LANES], vn, False))
            off += LANES
        m = parts[0][0].max(axis=-1, keepdims=True)
        for lg, _, _ in parts[1:]:
            m = jnp.maximum(m, lg.max(axis=-1, keepdims=True))
        s = jnp.zeros((qrows, 1), F32)
        acc = jnp.zeros((qrows, HEAD_DIM), F32)
        for lg, vv, cached in parts:
            p = jnp.exp(lg - m)
            s = s + jnp.sum(p, axis=-1, keepdims=True)
            if cached:
                p = pltpu.roll(p, GROUP_HEADS, axis=1)
            acc = acc + _dot(p.astype(BF16), vv)
        res = acc / s
        for h in range(GROUP_HEADS):
            o_ref[j, :, h * HEAD_DIM:(h + 1) * HEAD_DIM] = res[h * ts:(h + 1) * ts]


def _dil_sample_call(qkv, caches, l, bias_s, ns, ts, sb):
    c0, c1, c2 = caches
    return pl.pallas_call(
        _dil_sample_kernel,
        grid=(ns // sb,),
        in_specs=[pl.BlockSpec((None, N_QKV, sb * ts, HEAD_DIM), lambda i: (0, 0, i, 0)),
                  pl.BlockSpec((None, sb) + c0.shape[2:], lambda i: (l, i, 0, 0)),
                  pl.BlockSpec((None, sb) + c1.shape[2:], lambda i: (l, i, 0, 0)),
                  pl.BlockSpec((None, sb, c2.shape[2], ts * CACHE_ROW, LANES),
                               lambda i: (l, i, 0, 0, 0)),
                  pl.BlockSpec(bias_s.shape, lambda i: (0, 0))],
        out_specs=pl.BlockSpec((sb, ts, B_OUT), lambda i: (i, 0, 0)),
        out_shape=jax.ShapeDtypeStruct((ns, ts, B_OUT), F32),
        compiler_params=_cparams("arbitrary"),
        name="dilated_sample",
    )(qkv, c0, c1, c2, bias_s)


def _merge_kernel(*refs, combine):
    if combine:
        (x_ref, mod_ref, oa_ref, gr_ref, ga_ref, gb_ref, o0_ref, o1_ref, o2_ref,
         l0_ref, l1_ref, l2_ref, wpa_ref, wpb_ref, wo_ref, out_ref) = refs
    else:
        (x_ref, mod_ref, oa_ref, gr_ref, ga_ref, gb_ref, ob_ref,
         wpa_ref, wpb_ref, wo_ref, out_ref) = refs
    gb, rb, d = x_ref.shape
    tm = gb * rb
    if combine:
        o_refs = (o0_ref, o1_ref, o2_ref)
        l_refs = (l0_ref, l1_ref, l2_ref)
        parts = []
        for h in range(GROUP_HEADS):
            ls = [l[h] for l in l_refs]
            m = jnp.maximum(jnp.maximum(ls[0], ls[1]), ls[2])
            es = [jnp.exp(l - m) for l in ls]
            den = es[0] + es[1] + es[2]
            parts.append(sum((es[g] / den) * o_refs[g][h] for g in range(N_GROUPS)))
        ob = jnp.concatenate(parts, axis=1)
    else:
        ob = ob_ref[...].reshape(tm, B_OUT)
    gr = gr_ref[...].reshape(tm, GLA_DV)
    oa = oa_ref[...].reshape(tm, GLA_DV) * (gr * _sigmoid(gr))
    m1 = _dot(oa.astype(BF16), wpa_ref[...])
    m2 = _dot(ob.astype(BF16), wpb_ref[...])
    merged = (_sigmoid(ga_ref[...].reshape(tm, d)) * m1
              + _sigmoid(gb_ref[...].reshape(tm, d)) * m2)
    y = _dot(merged.astype(BF16), wo_ref[...]).reshape(gb, rb, d)
    out_ref[...] = x_ref[...] + mod_ref[...][:, 2:3, :] * y


def _merge_call(x, mod, oa, proj, ob_parts, wpa, wpb, wo, gb, rb):
    G, R, D = x.shape
    combine = len(ob_parts) > 1
    tm = gb * rb
    tok = lambda w, c: pl.BlockSpec((gb, rb, w), lambda a, b: (a, b, c))
    const = lambda shp: pl.BlockSpec(shp, lambda a, b: (0, 0))
    in_specs = [tok(D, 0), pl.BlockSpec((gb, N_MOD, D), lambda a, b: (a, 0, 0)),
                tok(GLA_DV, 0), tok(GLA_DV, 2), tok(D, 3), tok(D, 4)]
    args = [x, mod, oa, proj, proj, proj]
    if combine:
        os_, ls_ = ob_parts
        head_major = pl.BlockSpec((None, GROUP_HEADS, tm, HEAD_DIM), lambda a, b: (a, 0, b, 0))
        in_specs += [head_major] * (2 * N_GROUPS)
        args += list(os_) + list(ls_)
    else:
        in_specs += [tok(B_OUT, 0)]
        args += list(ob_parts)
    in_specs += [const((GLA_DV, D)), const((B_OUT, D)), const((D, D))]
    args += [wpa, wpb, wo]
    return pl.pallas_call(
        functools.partial(_merge_kernel, combine=combine),
        grid=(G // gb, R // rb),
        in_specs=in_specs,
        out_specs=tok(D, 0),
        out_shape=jax.ShapeDtypeStruct((G, R, D), F32),
        compiler_params=_cparams("arbitrary", "arbitrary"),
        name="merge_combine" if combine else "merge",
    )(*args)


def _mlp_kernel(x_ref, mod_ref, g_ref, wu_ref, wd_ref, out_ref, u_scr, acc_scr):
    f = pl.program_id(2)
    gb, rb, d = x_ref.shape
    tm = gb * rb

    @pl.when(f == 0)
    def _():
        mod = mod_ref[...]
        u = _norm_mod(x_ref[...], g_ref[...], mod[:, 3:4, :], mod[:, 4:5, :])
        u_scr[...] = u.reshape(tm, d).astype(BF16)
        acc_scr[...] = jnp.zeros_like(acc_scr)

    hid = jnp.maximum(_dot(u_scr[...], wu_ref[...]), 0.0)
    acc_scr[...] += _dot((hid * hid).astype(BF16), wd_ref[...])

    @pl.when(f == pl.num_programs(2) - 1)
    def _():
        out_ref[...] = x_ref[...] + mod_ref[...][:, 5:6, :] * acc_scr[...].reshape(gb, rb, d)


def _mlp_call(x, mod, g, wu, wd, gb, rb, tf):
    G, R, D = x.shape
    F = wu.shape[1]
    tm = gb * rb
    return pl.pallas_call(
        _mlp_kernel,
        grid=(G // gb, R // rb, F // tf),
        in_specs=[pl.BlockSpec((gb, rb, D), lambda a, b, f: (a, b, 0)),
                  pl.BlockSpec((gb, N_MOD, D), lambda a, b, f: (a, 0, 0)),
                  pl.BlockSpec((1, D), lambda a, b, f: (0, 0)),
                  pl.BlockSpec((D, tf), lambda a, b, f: (0, f)),
                  pl.BlockSpec((tf, D), lambda a, b, f: (f, 0))],
        out_specs=pl.BlockSpec((gb, rb, D), lambda a, b, f: (a, b, 0)),
        out_shape=jax.ShapeDtypeStruct((G, R, D), F32),
        scratch_shapes=[pltpu.VMEM((tm, D), BF16), pltpu.VMEM((tm, D), F32)],
        compiler_params=_cparams("arbitrary", "arbitrary", "arbitrary"),
        name="mlp",
    )(x, mod, g, wu, wd)


def _t5_bucket(dist):
    dist = np.asarray(dist)
    large = REL_MAX_EXACT + (np.log(np.maximum(dist, 1) / REL_MAX_EXACT)
                             / np.log(REL_MAX_DIST / REL_MAX_EXACT)
                             * (REL_BUCKETS - REL_MAX_EXACT)).astype(np.int32)
    large = np.minimum(large, REL_BUCKETS - 1)
    return np.where(dist < REL_MAX_EXACT, dist, large).astype(np.int32)


def _tap_bias(rel_bias, g):
    d = DIL_PAIRS[g][1]
    idx = _t5_bucket(d * np.arange(TAPS))
    return rel_bias[idx][:, g * GROUP_HEADS:(g + 1) * GROUP_HEADS].astype(F32)


def _prompt_bias(rel_bias, g):
    tb = _tap_bias(rel_bias, g)
    w = jnp.concatenate([tb[::-1].T, jnp.full((GROUP_HEADS, Q_BLOCK), NEG, F32)], axis=1)
    flat = jnp.tile(w, (1, Q_BLOCK))[:, :Q_BLOCK * 2 * Q_BLOCK]
    return flat.reshape(GROUP_HEADS, Q_BLOCK, 2 * Q_BLOCK)


def _dist_bias(tb, d, n):
    h = tb.shape[1]
    v = jnp.concatenate([tb[:, None, :], jnp.full((TAPS, d - 1, h), NEG, F32)], axis=1)
    v = v.reshape(TAPS * d, h)
    if n > TAPS * d:
        v = jnp.concatenate([v, jnp.full((n - TAPS * d, h), NEG, F32)], axis=0)
    return v[:n].T


def _sample_bias(rel_bias, ts, widths):
    H = GROUP_HEADS
    heads = np.arange(H)
    own = jnp.asarray(np.arange(CACHE_ROW)[None, :] == heads[:, None])
    same = jnp.asarray(heads[:, None] == heads[None, :])
    cache_parts, new_parts = [], []
    for g, (w, d) in enumerate(DIL_PAIRS):
        w_eff = widths[g]
        bv = _dist_bias(_tap_bias(rel_bias, g), d, w_eff + ts)
        rows = jnp.stack([bv[:, t + 1:t + 1 + w_eff][:, ::-1] for t in range(ts)], axis=1)
        if g == N_GROUPS - 1:
            rows = rows.reshape(H, ts, w_eff // d, d)[..., :ts].reshape(H, ts, (w_eff // d) * ts)
        tab = jnp.where(own[:, None, None, :], rows[..., None], NEG)
        cache_parts.append(tab.reshape(H * ts, -1))
        tri = jnp.stack([jnp.concatenate([bv[:, :t + 1][:, ::-1],
                                          jnp.full((H, ts - 1 - t), NEG, F32)], axis=1)
                         for t in range(ts)], axis=1)
        blk = jnp.where(same[:, None, :, None], tri[:, :, None, :], NEG).reshape(H * ts, H * ts)
        new_parts.append(jnp.concatenate([blk, jnp.full((H * ts, LANES - H * ts), NEG, F32)], axis=1))
    return jnp.concatenate(cache_parts + new_parts, axis=1)


def _window_rows(qkv, g, lo, lead):
    k0 = (SEG_BK - SEG_BQ + g) * GROUP_HEADS
    v0 = (SEG_BV - SEG_BQ + g) * GROUP_HEADS
    kv = jnp.stack([qkv[:, k0:k0 + GROUP_HEADS, lo:], qkv[:, v0:v0 + GROUP_HEADS, lo:]], axis=1)
    kv = jnp.transpose(kv, (0, 3, 1, 2, 4))
    return kv.reshape(lead + kv.shape[2:])


def kernel(x_prompt, x_sample, c_prompt, c_sample, state_gla, cache_win1, cache_win2, cache_win3,
           rel_bias, norm1_g, norm2_g, w_mod, b_mod, w_in, w_alpha, b_alpha, gla_norm_g,
           qn_g, kn_g, w_pa, w_pb, w_o, w_up, w_down):
    B, T, D = x_prompt.shape
    NS, TS, _ = x_sample.shape
    L = w_mod.shape[0]
    widths = [c.shape[2] for c in (cache_win1, cache_win2, cache_win3)]
    d2 = DIL_PAIRS[2][1]
    assert D == D_MODEL and TS == 8 and T % SUPER == 0
    assert widths[2] % d2 == 0 and TS <= d2
    caches = (cache_win1.astype(F32).reshape(L, NS, widths[0] * CACHE_ROW, LANES),
              cache_win2.astype(F32).reshape(L, NS, widths[1] * CACHE_ROW, LANES),
              cache_win3.astype(F32).reshape(L, NS, widths[2] // d2, d2 * CACHE_ROW, LANES))
    state = state_gla.astype(F32)

    mods = _mod_call(jnp.concatenate([c_prompt, c_sample], axis=0).astype(F32), w_mod, b_mod)
    mods = mods.reshape(L, B + NS, N_MOD, D)

    sizes = (GLA_DK, GLA_DK, GLA_DV, GLA_DV, GLA_RANK, B_WIDTH, B_WIDTH, B_WIDTH, D_MODEL, D_MODEL)
    offs = np.concatenate([[0], np.cumsum(sizes)])
    piece = lambda i: w_in[:, :, offs[i]:offs[i + 1]]
    w_main = jnp.concatenate([piece(i) for i in (0, 1, 2, 3, 8, 9, 5, 6, 7)], axis=-1).astype(BF16)
    w_glr = jnp.pad(piece(4), ((0, 0), (0, 0), (0, LANES - GLA_RANK))).astype(BF16)
    w_al = jnp.pad(w_alpha, ((0, 0), (0, LANES - GLA_RANK), (0, 0))).astype(BF16)
    w_pa_b, w_pb_b, w_o_b = w_pa.astype(BF16), w_pb.astype(BF16), w_o.astype(BF16)
    w_up_b, w_down_b = w_up.astype(BF16), w_down.astype(BF16)

    bias_p = [_prompt_bias(rel_bias, g) for g in range(N_GROUPS)]
    bias_s = _sample_bias(rel_bias, TS, widths)

    tm = 1024
    xp = x_prompt.astype(F32)
    xs = x_sample.astype(F32)
    gla_p = []
    gla_s_all = None
    win_p = [[] for _ in range(N_GROUPS)]
    win_s = [[] for _ in range(N_GROUPS)]
    row2 = lambda v: v.reshape(1, -1)
    for l in range(L):
        mod_p, mod_s = mods[l, :B], mods[l, B:]
        pre = (row2(norm1_g[l]), w_glr[l], w_al[l], row2(b_alpha[l]))
        qk_gain = (row2(qn_g[l]), row2(kn_g[l]))
        gn = row2(gla_norm_g[l])

        u, la = _prenorm_call(xp, mod_p, *pre, gb=1, rb=tm)
        proj, qkv = _inproj_call(u, w_main[l], *qk_gain, tm=2 * tm)
        oa, s_fin = _gla_prompt_call(proj, la, gn, tb=256)
        outs = [_dil_prompt_call(qkv, bias_p[g], g, DIL_PAIRS[g][1]) for g in range(N_GROUPS)]
        xp = _merge_call(xp, mod_p, oa, proj, ([o for o, _ in outs], [s for _, s in outs]),
                         w_pa_b[l], w_pb_b[l], w_o_b[l], gb=1, rb=tm // 2)
        xp = _mlp_call(xp, mod_p, row2(norm2_g[l]), w_up_b[l], w_down_b[l], gb=1, rb=tm, tf=1024)
        gla_p.append(s_fin)
        for g, (w, d) in enumerate(DIL_PAIRS):
            keep = min(w, T)
            win_p[g].append(_window_rows(qkv, g, T - keep, (B, keep)))

        u, la = _prenorm_call(xs, mod_s, *pre, gb=NS, rb=TS)
        proj, qkv = _inproj_call(u, w_main[l], *qk_gain, tm=NS * TS)
        proj = proj.reshape(NS, TS, MAIN_W)
        oa, gla_s_all = _gla_sample_call(proj, la, state, l, gn, sb=8, acc=gla_s_all)
        ob = _dil_sample_call(qkv, caches, l, bias_s, NS, TS, sb=2)
        xs = _merge_call(xs, mod_s, oa, proj, (ob,), w_pa_b[l], w_pb_b[l], w_o_b[l], gb=NS, rb=TS)
        xs = _mlp_call(xs, mod_s, row2(norm2_g[l]), w_up_b[l], w_down_b[l], gb=NS, rb=TS, tf=1024)
        for g in range(N_GROUPS):
            win_s[g].append(_window_rows(qkv, g, 0, (NS, TS)))

    return (xp.astype(x_prompt.dtype), xs.astype(x_sample.dtype), jnp.stack(gla_p), gla_s_all,
            jnp.stack(win_p[0]), jnp.stack(win_s[0]), jnp.stack(win_p[1]), jnp.stack(win_s[1]),
            jnp.stack(win_p[2]), jnp.stack(win_s[2]))
```

```python
import functools

import numpy as np
import jax
import jax.numpy as jnp
from jax import lax
from jax.experimental import pallas as pl
from jax.experimental.pallas import tpu as pltpu

F32 = jnp.float32
BF16 = jnp.bfloat16

D_MODEL = 1024
GLA_HEADS = 4
GLA_DK_HEAD = 128
GLA_DV_HEAD = 256
GLA_DK = GLA_HEADS * GLA_DK_HEAD
GLA_DV = GLA_HEADS * GLA_DV_HEAD
GLA_RANK = 16
GLA_GATE_NORM = 16.0
GLA_CHUNK = 128
DIL_PAIRS = ((128, 1), (512, 4), (2048, 16))
N_GROUPS = 3
GROUP_HEADS = 4
HEAD_DIM = 128
B_WIDTH = N_GROUPS * GROUP_HEADS * HEAD_DIM
B_OUT = GROUP_HEADS * HEAD_DIM
TAPS = 129
Q_BLOCK = 128
REL_BUCKETS = 32
REL_MAX_EXACT = 16
REL_MAX_DIST = 2048
D_FF = 4 * D_MODEL
N_MOD = 6
EPS = 1e-6
NEG = -1e30

LANES = 128
CACHE_ROW = 2 * GROUP_HEADS

SEG = 512
SEG_GATE = 6
SEG_MAIN = 10
NSEG = 19
MAIN_W = SEG_MAIN * SEG
SEG_BQ, SEG_BK, SEG_BV = 10, 13, 16
N_QKV = (NSEG - SEG_MAIN) * GROUP_HEADS
SUPER = 2048

VMEM_LIMIT = 48 * 1024 * 1024


def _cparams(*sem):
    return pltpu.CompilerParams(dimension_semantics=sem, vmem_limit_bytes=VMEM_LIMIT)


def _dot(a, b):
    return jnp.dot(a, b, preferred_element_type=F32)


def _dot_nt(a, b):
    return lax.dot_general(a, b, (((1,), (1,)), ((), ())), preferred_element_type=F32)


def _dot_tn(a, b):
    return lax.dot_general(a, b, (((0,), (0,)), ((), ())), preferred_element_type=F32)


def _sigmoid(x):
    return 1.0 / (1.0 + jnp.exp(-x))


def _norm_mod(x, g, shift, scale):
    ms = jnp.mean(x * x, axis=-1, keepdims=True)
    n = x * lax.rsqrt(ms + EPS) * g
    return n * (1.0 + scale) + shift


def _rms_lanes(x, g):
    ms = jnp.mean(x * x, axis=-1, keepdims=True)
    return x * lax.rsqrt(ms + EPS) * g


def _split2(x):
    hi = x.astype(BF16)
    lo = (x - hi.astype(F32)).astype(BF16)
    return hi, lo


def _mod_kernel(c_ref, w_ref, b_ref, o_ref):
    c = c_ref[...]
    a = (c * _sigmoid(c)).astype(BF16)
    o_ref[...] = _dot(a, w_ref[...].astype(BF16)) + b_ref[...]


def _mod_call(c_all, w_mod, b_mod):
    L, D, N = w_mod.shape
    nb = c_all.shape[0]
    tn = 1024
    return pl.pallas_call(
        _mod_kernel,
        grid=(L, N // tn),
        in_specs=[pl.BlockSpec((nb, D), lambda l, n: (0, 0)),
                  pl.BlockSpec((None, D, tn), lambda l, n: (l, 0, n)),
                  pl.BlockSpec((None, 1, tn), lambda l, n: (l, 0, n))],
        out_specs=pl.BlockSpec((None, nb, tn), lambda l, n: (l, 0, n)),
        out_shape=jax.ShapeDtypeStruct((L, nb, N), F32),
        compiler_params=_cparams("arbitrary", "arbitrary"),
        name="adaln_mod",
    )(c_all, w_mod, b_mod.reshape(L, 1, N))


def _prenorm_kernel(x_ref, mod_ref, g_ref, wglr_ref, wal_ref, bal_ref, u_ref, la_ref):
    gb, rb, d = x_ref.shape
    tm = gb * rb
    mod = mod_ref[...]
    u = _norm_mod(x_ref[...], g_ref[...], mod[:, 0:1, :], mod[:, 1:2, :])
    ub = u.reshape(tm, d).astype(BF16)
    u_ref[...] = ub
    glr = _dot(ub, wglr_ref[...])
    z = _dot(glr.astype(BF16), wal_ref[...]) + bal_ref[...]
    la = (jnp.minimum(z, 0.0) - jnp.log1p(jnp.exp(-jnp.abs(z)))) * (1.0 / GLA_GATE_NORM)
    la_ref[...] = la.reshape(gb, rb, GLA_DK)


def _prenorm_call(x, mod, g, wglr, wal, l, bal, gb, rb):
    G, R, D = x.shape
    tm = gb * rb
    na, nr = G // gb, R // rb
    const2 = lambda a, b: (0, 0)
    return pl.pallas_call(
        _prenorm_kernel,
        grid=(na, nr),
        in_specs=[pl.BlockSpec((gb, rb, D), lambda a, b: (a, b, 0)),
                  pl.BlockSpec((gb, N_MOD, D), lambda a, b: (a, 0, 0)),
                  pl.BlockSpec((1, D), const2),
                  pl.BlockSpec((None, D, LANES), lambda a, b: (l, 0, 0)),
                  pl.BlockSpec((None, LANES, GLA_DK), lambda a, b: (l, 0, 0)),
                  pl.BlockSpec((1, GLA_DK), const2)],
        out_specs=[pl.BlockSpec((None, tm, D), lambda a, b: (a, b, 0)),
                   pl.BlockSpec((gb, rb, GLA_DK), lambda a, b: (a, b, 0))],
        out_shape=[jax.ShapeDtypeStruct((na, nr * tm, D), BF16),
                   jax.ShapeDtypeStruct((G, R, GLA_DK), F32)],
        compiler_params=_cparams("arbitrary", "arbitrary"),
        name="prenorm",
    )(x, mod, g, wglr, wal, bal)


def _inproj_kernel(u_ref, wg_ref, wt_ref, wb_ref, qn_ref, kn_ref, proj_ref, qkv_ref):
    s = pl.program_id(2)
    tm = u_ref.shape[0]
    rc = min(tm, 512)

    @pl.when(s < SEG_GATE)
    def _():
        proj_ref[...] = _dot(u_ref[...], wg_ref[...])

    @pl.when((s >= SEG_GATE) & (s < SEG_MAIN))
    def _():
        proj_ref[...] = _dot(u_ref[...], wt_ref[...])

    def to_heads(r0, rows, val):
        for h in range(GROUP_HEADS):
            qkv_ref[h, r0:r0 + rows, :] = val[:, h * HEAD_DIM:(h + 1) * HEAD_DIM]

    def normed(gain_ref):
        g = gain_ref[...]
        for c in range(tm // rc):
            r = _dot(u_ref[c * rc:(c + 1) * rc, :], wb_ref[...])
            to_heads(c * rc, rc, jnp.concatenate(
                [_rms_lanes(r[:, h * HEAD_DIM:(h + 1) * HEAD_DIM], g) for h in range(GROUP_HEADS)],
                axis=1))

    @pl.when((s >= SEG_BQ) & (s < SEG_BK))
    def _():
        normed(qn_ref)

    @pl.when((s >= SEG_BK) & (s < SEG_BV))
    def _():
        normed(kn_ref)

    @pl.when(s >= SEG_BV)
    def _():
        to_heads(0, tm, _dot(u_ref[...], wb_ref[...]))


def _inproj_call(u, w_gla, w_gate, w_b, l, qn, kn, tm):
    A, N, D = u.shape
    const2 = lambda a, b, s: (0, 0)

    def wspec(first, count):
        return pl.BlockSpec((None, D, SEG), lambda a, b, s: (l, 0, jnp.clip(s - first, 0, count - 1)))

    return pl.pallas_call(
        _inproj_kernel,
        grid=(A, N // tm, NSEG),
        in_specs=[pl.BlockSpec((None, tm, D), lambda a, b, s: (a, b, 0)),
                  wspec(0, SEG_GATE), wspec(SEG_GATE, SEG_MAIN - SEG_GATE),
                  wspec(SEG_BQ, NSEG - SEG_BQ),
                  pl.BlockSpec((1, HEAD_DIM), const2),
                  pl.BlockSpec((1, HEAD_DIM), const2)],
        out_specs=[pl.BlockSpec((None, tm, SEG), lambda a, b, s: (a, b, jnp.minimum(s, SEG_MAIN - 1))),
                   pl.BlockSpec((None, GROUP_HEADS, tm, HEAD_DIM),
                                lambda a, b, s: (a, jnp.clip(s - SEG_BQ, 0, NSEG - SEG_BQ - 1), b, 0))],
        out_shape=[jax.ShapeDtypeStruct((A, N, MAIN_W), F32),
                   jax.ShapeDtypeStruct((A, N_QKV, N, HEAD_DIM), F32)],
        compiler_params=_cparams("arbitrary", "arbitrary", "arbitrary"),
        name="in_proj",
    )(u, w_gla, w_gate, w_b, qn, kn)


def _gla_unit(q, k, v, la, S, tril, lmat, ones_c):
    la_hi, la_lo = _split2(la)
    b = _dot(lmat, la_hi) + _dot(lmat, la_lo)
    half = b.shape[0] // 2
    b_mid = b[half - 1:half, :]
    qs = q * (GLA_DK_HEAD ** -0.5)
    qe = (qs * jnp.exp(b)).astype(BF16)
    qm = (qs * jnp.exp(b - b_mid)).astype(BF16)
    km = (k * jnp.exp(b_mid - b)).astype(BF16)
    vb = v.astype(BF16)
    sc = jnp.where(tril, _dot_nt(qm, km), 0.0)
    o = _dot(qe, S.astype(BF16)) + _dot(sc.astype(BF16), vb)
    b_last = b[b.shape[0] - 1:, :]
    kd = (k * jnp.exp(b_last - b)).astype(BF16)
    dcol = _dot_tn(la_hi, ones_c) + _dot_tn(la_lo, ones_c)
    dec = jnp.exp(dcol)
    s_new = S * jnp.concatenate([dec, dec], axis=1) + _dot_tn(kd, vb)
    return o, s_new


def _gla_prompt_kernel(q_ref, k_ref, v_ref, la_ref, gn_ref, o_ref, sfin_ref, s_scr):
    j = pl.program_id(0)
    nb, tb, _ = q_ref.shape
    C = GLA_CHUNK

    @pl.when(j == 0)
    def _():
        s_scr[...] = jnp.zeros_like(s_scr)

    row = lax.broadcasted_iota(jnp.int32, (C, C), 0)
    col = lax.broadcasted_iota(jnp.int32, (C, C), 1)
    tril = col <= row
    lmat = tril.astype(BF16)
    ones_c = jnp.ones((C, LANES), BF16)
    gn = gn_ref[...]
    for b in range(nb):
        for h in range(GLA_HEADS):
            ksl = slice(h * GLA_DK_HEAD, (h + 1) * GLA_DK_HEAD)
            vsl = slice(h * GLA_DV_HEAD, (h + 1) * GLA_DV_HEAD)
            S = s_scr[b, h]
            for c in range(tb // C):
                rows = slice(c * C, (c + 1) * C)
                o, S = _gla_unit(q_ref[b, rows, ksl], k_ref[b, rows, ksl], v_ref[b, rows, vsl],
                                 la_ref[b, rows, ksl], S, tril, lmat, ones_c)
                o_ref[b, rows, vsl] = _rms_lanes(o, gn)
            s_scr[b, h] = S

    @pl.when(j == pl.num_programs(0) - 1)
    def _():
        sfin_ref[...] = s_scr[...]


def _gla_prompt_call(proj, la, gn, tb):
    B, T, _ = proj.shape
    st = (B, GLA_HEADS, GLA_DK_HEAD, GLA_DV_HEAD)
    return pl.pallas_call(
        _gla_prompt_kernel,
        grid=(T // tb,),
        in_specs=[pl.BlockSpec((B, tb, GLA_DK), lambda j: (0, j, 0)),
                  pl.BlockSpec((B, tb, GLA_DK), lambda j: (0, j, 1)),
                  pl.BlockSpec((B, tb, GLA_DV), lambda j: (0, j, 1)),
                  pl.BlockSpec((B, tb, GLA_DK), lambda j: (0, j, 0)),
                  pl.BlockSpec((1, GLA_DV_HEAD), lambda j: (0, 0))],
        out_specs=[pl.BlockSpec((B, tb, GLA_DV), lambda j: (0, j, 0)),
                   pl.BlockSpec(st, lambda j: (0, 0, 0, 0))],
        out_shape=[jax.ShapeDtypeStruct((B, T, GLA_DV), F32), jax.ShapeDtypeStruct(st, F32)],
        scratch_shapes=[pltpu.VMEM(st, F32)],
        compiler_params=_cparams("arbitrary"),
        name="gla_prompt",
    )(proj, proj, proj, la, gn)


def _gla_sample_kernel(q_ref, k_ref, v_ref, la_ref, sin_ref, gn_ref, *rest):
    o_ref, sout_ref = rest[-2:]
    sb, ts, _ = q_ref.shape
    R = sb * ts
    row = lax.broadcasted_iota(jnp.int32, (R, R), 0)
    col = lax.broadcasted_iota(jnp.int32, (R, R), 1)
    shift = ts.bit_length() - 1
    same = jnp.right_shift(row, shift) == jnp.right_shift(col, shift)
    tril = same & (col <= row)
    lmat = tril.astype(BF16)
    bones = same.astype(BF16)
    ones_t = jnp.ones((ts, LANES), F32)
    gn = gn_ref[...]
    for h in range(GLA_HEADS):
        ksl = slice(h * GLA_DK_HEAD, (h + 1) * GLA_DK_HEAD)
        vsl = slice(h * GLA_DV_HEAD, (h + 1) * GLA_DV_HEAD)
        q = q_ref[:, :, ksl].reshape(R, GLA_DK_HEAD) * (GLA_DK_HEAD ** -0.5)
        k = k_ref[:, :, ksl].reshape(R, GLA_DK_HEAD)
        v = v_ref[:, :, vsl].reshape(R, GLA_DV_HEAD)
        la = la_ref[:, :, ksl].reshape(R, GLA_DK_HEAD)
        la_hi, la_lo = _split2(la)
        b = _dot(lmat, la_hi) + _dot(lmat, la_lo)
        btot = _dot(bones, la_hi) + _dot(bones, la_lo)
        qe = q * jnp.exp(b)
        ke = (k * jnp.exp(-b)).astype(BF16)
        kd = k * jnp.exp(btot - b)
        vb = v.astype(BF16)
        sc = jnp.where(tril, _dot_nt(qe.astype(BF16), ke), 0.0)
        o_intra = _dot(sc.astype(BF16), vb)
        la_hi32, la_lo32 = la_hi.astype(F32), la_lo.astype(F32)
        for i in range(sb):
            rs = slice(i * ts, (i + 1) * ts)
            S = sin_ref[i, h]
            o = _dot(qe[rs].astype(BF16), S.astype(BF16)) + o_intra[rs]
            kv = _dot_tn(kd[rs].astype(BF16), v[rs].astype(BF16))
            dec = jnp.exp(_dot_tn(la_hi32[rs].astype(BF16), ones_t.astype(BF16))
                          + _dot_tn(la_lo32[rs].astype(BF16), ones_t.astype(BF16)))
            sout_ref[i, h] = S * jnp.concatenate([dec, dec], axis=1) + kv
            o_ref[i, :, vsl] = _rms_lanes(o, gn)


def _gla_sample_call(proj, la, state, l, gn, sb, acc):
    NS, TS, _ = proj.shape
    st_shape = (None, sb, GLA_HEADS, GLA_DK_HEAD, GLA_DV_HEAD)
    st_spec = pl.BlockSpec(st_shape, lambda i: (l, i, 0, 0, 0))
    in_specs = [pl.BlockSpec((sb, TS, GLA_DK), lambda i: (i, 0, 0)),
                pl.BlockSpec((sb, TS, GLA_DK), lambda i: (i, 0, 1)),
                pl.BlockSpec((sb, TS, GLA_DV), lambda i: (i, 0, 1)),
                pl.BlockSpec((sb, TS, GLA_DK), lambda i: (i, 0, 0)),
                st_spec,
                pl.BlockSpec((1, GLA_DV_HEAD), lambda i: (0, 0))]
    args = [proj, proj, proj, la, state, gn]
    aliases = {}
    if acc is not None:
        in_specs.append(pl.BlockSpec(memory_space=pl.ANY))
        args.append(acc)
        aliases = {len(args) - 1: 1}
    return pl.pallas_call(
        _gla_sample_kernel,
        grid=(NS // sb,),
        in_specs=in_specs,
        out_specs=[pl.BlockSpec((sb, TS, GLA_DV), lambda i: (i, 0, 0)), st_spec],
        out_shape=[jax.ShapeDtypeStruct((NS, TS, GLA_DV), F32),
                   jax.ShapeDtypeStruct(state.shape, F32)],
        input_output_aliases=aliases,
        compiler_params=_cparams("arbitrary"),
        name="gla_sample",
    )(*args)


def _dil_prompt_kernel(q_ref, k_ref, v_ref, bias_ref, o_ref, lse_ref, kc_scr, vc_scr, *, d):
    n = pl.program_id(1)
    h = pl.program_id(2)
    nsub = SUPER // (Q_BLOCK * d)
    scale = HEAD_DIM ** -0.5

    @pl.when(n == 0)
    def _():
        kc_scr[h] = jnp.zeros(kc_scr.shape[1:], BF16)
        vc_scr[h] = jnp.zeros(vc_scr.shape[1:], BF16)

    bias = bias_ref[...]
    col = lax.broadcasted_iota(jnp.int32, bias.shape, 1)
    bias_first = bias + jnp.where((col < Q_BLOCK) & (n == 0), NEG, 0.0)
    ones = jnp.ones((Q_BLOCK, LANES), BF16)
    for p in range(d):
        cs = slice(p * Q_BLOCK, (p + 1) * Q_BLOCK)
        kp = kc_scr[h, cs, :]
        vp = vc_scr[h, cs, :]
        for i in range(nsub):
            start = p + d * Q_BLOCK * i
            rows = pl.ds(start, Q_BLOCK, stride=d) if d > 1 else pl.ds(start, Q_BLOCK)
            q = q_ref[rows, :].astype(BF16)
            kc = k_ref[rows, :].astype(BF16)
            vc = v_ref[rows, :].astype(BF16)
            lg = (_dot_nt(q, jnp.concatenate([kp, kc], axis=0)) * scale
                  + (bias_first if i == 0 else bias))
            m = jnp.max(lg, axis=-1, keepdims=True)
            pr = jnp.exp(lg - m).astype(BF16)
            v2 = jnp.concatenate([jnp.concatenate([vp, ones], axis=1),
                                  jnp.concatenate([vc, ones], axis=1)], axis=0)
            acc = _dot(pr, v2)
            s = acc[:, HEAD_DIM:]
            o_ref[rows, :] = acc[:, :HEAD_DIM] / s
            lse_ref[rows, :] = m + jnp.log(s)
            kp, vp = kc, vc
        kc_scr[h, cs, :] = kp
        vc_scr[h, cs, :] = vp


def _dil_prompt_call(qkv, bias_g, g, d):
    B, _, T, _ = qkv.shape
    plane = lambda seg: (seg - SEG_BQ + g) * GROUP_HEADS

    def spec(seg):
        base = plane(seg)
        return pl.BlockSpec((None, None, SUPER, HEAD_DIM), lambda b, n, h: (b, base + h, n, 0))

    out_spec = pl.BlockSpec((None, None, SUPER, HEAD_DIM), lambda b, n, h: (b, h, n, 0))
    out_sds = jax.ShapeDtypeStruct((B, GROUP_HEADS, T, HEAD_DIM), F32)
    carry = pltpu.VMEM((GROUP_HEADS, d * Q_BLOCK, HEAD_DIM), BF16)
    return pl.pallas_call(
        functools.partial(_dil_prompt_kernel, d=d),
        grid=(B, T // SUPER, GROUP_HEADS),
        in_specs=[spec(SEG_BQ), spec(SEG_BK), spec(SEG_BV),
                  pl.BlockSpec((None, Q_BLOCK, 2 * Q_BLOCK), lambda b, n, h: (h, 0, 0))],
        out_specs=[out_spec, out_spec],
        out_shape=[out_sds, out_sds],
        scratch_shapes=[carry, carry],
        compiler_params=_cparams("arbitrary", "arbitrary", "arbitrary"),
        name="dilated_prompt_g%d" % g,
    )(qkv, qkv, qkv, bias_g)


def _dil_sample_kernel(qkv_ref, c0_ref, c1_ref, c2_ref, bias_ref, o_ref):
    sb, ts, _ = o_ref.shape
    scale = HEAD_DIM ** -0.5
    qrows = GROUP_HEADS * ts
    n2 = c2_ref.shape[1] * c2_ref.shape[2]
    zpad = jnp.zeros((LANES - qrows, HEAD_DIM), F32)
    for j in range(sb):
        xs = (c0_ref[j].astype(BF16), c1_ref[j].astype(BF16),
              c2_ref[j].reshape(n2, LANES).astype(BF16))

        def planes(seg):
            return jnp.concatenate([qkv_ref[seg * GROUP_HEADS + h, j * ts:(j + 1) * ts, :]
                                    for h in range(GROUP_HEADS)], axis=0)

        qs = [planes(g).astype(BF16) for g in range(N_GROUPS)]
        parts = []
        off = 0
        for g in range(N_GROUPS):
            w = xs[g].shape[0]
            parts.append((_dot_nt(qs[g], xs[g]) * scale + bias_ref[:, off:off + w], xs[g], True))
            off += w
        for g in range(N_GROUPS):
            kn = jnp.concatenate([planes(N_GROUPS + g), zpad], axis=0).astype(BF16)
            vn = jnp.concatenate([planes(2 * N_GROUPS + g), zpad], axis=0).astype(BF16)
            parts.append((_dot_nt(qs[g], kn) * scale + bias_ref[:, off:off + LANES], vn, False))
            off += LANES
        m = parts[0][0].max(axis=-1, keepdims=True)
        for lg, _, _ in parts[1:]:
            m = jnp.maximum(m, lg.max(axis=-1, keepdims=True))
        s = jnp.zeros((qrows, 1), F32)
        acc = jnp.zeros((qrows, HEAD_DIM), F32)
        for lg, vv, cached in parts:
            p = jnp.exp(lg - m)
            s = s + jnp.sum(p, axis=-1, keepdims=True)
            if cached:
                p = pltpu.roll(p, GROUP_HEADS, axis=1)
            acc = acc + _dot(p.astype(BF16), vv)
        res = acc / s
        for h in range(GROUP_HEADS):
            o_ref[j, :, h * HEAD_DIM:(h + 1) * HEAD_DIM] = res[h * ts:(h + 1) * ts]


def _dil_sample_specs(caches, l, bias_s, ts, sb):
    c0, c1, c2 = caches
    in_specs = [pl.BlockSpec((None, N_QKV, sb * ts, HEAD_DIM), lambda i: (0, 0, i, 0)),
                pl.BlockSpec((None, sb) + c0.shape[2:], lambda i: (l, i, 0, 0)),
                pl.BlockSpec((None, sb) + c1.shape[2:], lambda i: (l, i, 0, 0)),
                pl.BlockSpec((None, sb, c2.shape[2], ts * CACHE_ROW, LANES),
                             lambda i: (l, i, 0, 0, 0)),
                pl.BlockSpec(bias_s.shape, lambda i: (0, 0))]
    return in_specs, pl.BlockSpec((sb, ts, B_OUT), lambda i: (i, 0, 0))


def _dil_sample_call(qkv, caches, l, bias_s, ns, ts, sb):
    in_specs, out_spec = _dil_sample_specs(caches, l, bias_s, ts, sb)
    return pl.pallas_call(
        _dil_sample_kernel,
        grid=(ns // sb,),
        in_specs=in_specs,
        out_specs=out_spec,
        out_shape=jax.ShapeDtypeStruct((ns, ts, B_OUT), F32),
        compiler_params=_cparams("arbitrary"),
        name="dilated_sample",
    )(qkv, *caches, bias_s)


def _merge_kernel(*refs, combine):
    if combine:
        (x_ref, mod_ref, oa_ref, gr_ref, ga_ref, gb_ref, o0_ref, o1_ref, o2_ref,
         l0_ref, l1_ref, l2_ref, wpa_ref, wpb_ref, wo_ref, out_ref) = refs
    else:
        (x_ref, mod_ref, oa_ref, gr_ref, ga_ref, gb_ref, ob_ref,
         wpa_ref, wpb_ref, wo_ref, out_ref) = refs
    gb, rb, d = x_ref.shape
    tm = gb * rb
    if combine:
        o_refs = (o0_ref, o1_ref, o2_ref)
        l_refs = (l0_ref, l1_ref, l2_ref)
        parts = []
        for h in range(GROUP_HEADS):
            ls = [l[h] for l in l_refs]
            m = jnp.maximum(jnp.maximum(ls[0], ls[1]), ls[2])
            es = [jnp.exp(l - m) for l in ls]
            den = es[0] + es[1] + es[2]
            parts.append(sum((es[g] / den) * o_refs[g][h] for g in range(N_GROUPS)))
        ob = jnp.concatenate(parts, axis=1)
    else:
        ob = ob_ref[...].reshape(tm, B_OUT)
    gr = gr_ref[...].reshape(tm, GLA_DV)
    oa = oa_ref[...].reshape(tm, GLA_DV) * (gr * _sigmoid(gr))
    m1 = _dot(oa.astype(BF16), wpa_ref[...])
    m2 = _dot(ob.astype(BF16), wpb_ref[...])
    merged = (_sigmoid(ga_ref[...].reshape(tm, d)) * m1
              + _sigmoid(gb_ref[...].reshape(tm, d)) * m2)
    y = _dot(merged.astype(BF16), wo_ref[...]).reshape(gb, rb, d)
    out_ref[...] = x_ref[...] + mod_ref[...][:, 2:3, :] * y


def _merge_call(x, mod, oa, proj, ob_parts, wpa, wpb, wo, l, gb, rb):
    G, R, D = x.shape
    combine = len(ob_parts) > 1
    tm = gb * rb
    tok = lambda w, c: pl.BlockSpec((gb, rb, w), lambda a, b: (a, b, c))
    const = lambda shp: pl.BlockSpec((None,) + shp, lambda a, b: (l, 0, 0))
    in_specs = [tok(D, 0), pl.BlockSpec((gb, N_MOD, D), lambda a, b: (a, 0, 0)),
                tok(GLA_DV, 0), tok(GLA_DV, 2), tok(D, 3), tok(D, 4)]
    args = [x, mod, oa, proj, proj, proj]
    if combine:
        os_, ls_ = ob_parts
        head_major = pl.BlockSpec((None, GROUP_HEADS, tm, HEAD_DIM), lambda a, b: (a, 0, b, 0))
        in_specs += [head_major] * (2 * N_GROUPS)
        args += list(os_) + list(ls_)
    else:
        in_specs += [tok(B_OUT, 0)]
        args += list(ob_parts)
    in_specs += [const((GLA_DV, D)), const((B_OUT, D)), const((D, D))]
    args += [wpa, wpb, wo]
    return pl.pallas_call(
        functools.partial(_merge_kernel, combine=combine),
        grid=(G // gb, R // rb),
        in_specs=in_specs,
        out_specs=tok(D, 0),
        out_shape=jax.ShapeDtypeStruct((G, R, D), F32),
        compiler_params=_cparams("arbitrary", "arbitrary"),
        name="merge_combine" if combine else "merge",
    )(*args)


def _mlp_kernel(x_ref, mod_ref, g_ref, wu_ref, wd_ref, *rest, with_sample):
    if with_sample:
        qkv_ref, c0_ref, c1_ref, c2_ref, bias_ref, out_ref, ob_ref, u_scr, acc_scr = rest
    else:
        out_ref, u_scr, acc_scr = rest
    f = pl.program_id(2)
    gb, rb, d = x_ref.shape
    tm = gb * rb

    @pl.when(f == 0)
    def _():
        mod = mod_ref[...]
        u = _norm_mod(x_ref[...], g_ref[...], mod[:, 3:4, :], mod[:, 4:5, :])
        u_scr[...] = u.reshape(tm, d).astype(BF16)
        acc_scr[...] = jnp.zeros_like(acc_scr)

    hid = jnp.maximum(_dot(u_scr[...], wu_ref[...]), 0.0)
    acc_scr[...] += _dot((hid * hid).astype(BF16), wd_ref[...])
    if with_sample:
        _dil_sample_kernel(qkv_ref, c0_ref, c1_ref, c2_ref, bias_ref, ob_ref)

    @pl.when(f == pl.num_programs(2) - 1)
    def _():
        out_ref[...] = x_ref[...] + mod_ref[...][:, 5:6, :] * acc_scr[...].reshape(gb, rb, d)


def _mlp_call(x, mod, g, wu, wd, l, gb, rb, tf, sample=None):
    G, R, D = x.shape
    F = wu.shape[2]
    tm = gb * rb
    nr, nf = R // rb, F // tf
    in_specs = [pl.BlockSpec((gb, rb, D), lambda a, b, f: (a, b, 0)),
                pl.BlockSpec((gb, N_MOD, D), lambda a, b, f: (a, 0, 0)),
                pl.BlockSpec((1, D), lambda a, b, f: (0, 0)),
                pl.BlockSpec((None, D, tf), lambda a, b, f: (l, 0, f)),
                pl.BlockSpec((None, tf, D), lambda a, b, f: (l, f, 0))]
    args = [x, mod, g, wu, wd]
    out_specs = [pl.BlockSpec((gb, rb, D), lambda a, b, f: (a, b, 0))]
    out_shape = [jax.ShapeDtypeStruct((G, R, D), F32)]
    if sample is not None:
        qkv, caches, l, bias_s, ns, ts = sample
        assert (G // gb) * nr * nf == ns
        step = lambda a, b, f: (a * nr + b) * nf + f
        ss = _dil_sample_specs(caches, l, bias_s, ts, 1)
        in_specs += [pl.BlockSpec(s.block_shape, (lambda im: lambda a, b, f: im(step(a, b, f)))(s.index_map))
                     for s in ss[0]]
        args += [qkv, *caches, bias_s]
        out_specs.append(pl.BlockSpec(ss[1].block_shape, lambda a, b, f: ss[1].index_map(step(a, b, f))))
        out_shape.append(jax.ShapeDtypeStruct((ns, ts, B_OUT), F32))
    res = pl.pallas_call(
        functools.partial(_mlp_kernel, with_sample=sample is not None),
        grid=(G // gb, nr, nf),
        in_specs=in_specs,
        out_specs=out_specs,
        out_shape=out_shape,
        scratch_shapes=[pltpu.VMEM((tm, D), BF16), pltpu.VMEM((tm, D), F32)],
        compiler_params=_cparams("arbitrary", "arbitrary", "arbitrary"),
        name="mlp" if sample is None else "mlp_dilated_sample",
    )(*args)
    return res[0] if sample is None else res


def _t5_bucket(dist):
    dist = np.asarray(dist)
    large = REL_MAX_EXACT + (np.log(np.maximum(dist, 1) / REL_MAX_EXACT)
                             / np.log(REL_MAX_DIST / REL_MAX_EXACT)
                             * (REL_BUCKETS - REL_MAX_EXACT)).astype(np.int32)
    large = np.minimum(large, REL_BUCKETS - 1)
    return np.where(dist < REL_MAX_EXACT, dist, large).astype(np.int32)


def _tap_bias(rel_bias, g):
    d = DIL_PAIRS[g][1]
    idx = _t5_bucket(d * np.arange(TAPS))
    return rel_bias[idx][:, g * GROUP_HEADS:(g + 1) * GROUP_HEADS].astype(F32)


def _prompt_bias(rel_bias, g):
    tb = _tap_bias(rel_bias, g)
    w = jnp.concatenate([tb[::-1].T, jnp.full((GROUP_HEADS, Q_BLOCK), NEG, F32)], axis=1)
    flat = jnp.tile(w, (1, Q_BLOCK))[:, :Q_BLOCK * 2 * Q_BLOCK]
    return flat.reshape(GROUP_HEADS, Q_BLOCK, 2 * Q_BLOCK)


def _dist_bias(tb, d, n):
    h = tb.shape[1]
    v = jnp.concatenate([tb[:, None, :], jnp.full((TAPS, d - 1, h), NEG, F32)], axis=1)
    v = v.reshape(TAPS * d, h)
    if n > TAPS * d:
        v = jnp.concatenate([v, jnp.full((n - TAPS * d, h), NEG, F32)], axis=0)
    return v[:n].T


def _sample_bias(rel_bias, ts, widths):
    H = GROUP_HEADS
    heads = np.arange(H)
    own = jnp.asarray(np.arange(CACHE_ROW)[None, :] == heads[:, None])
    same = jnp.asarray(heads[:, None] == heads[None, :])
    cache_parts, new_parts = [], []
    for g, (w, d) in enumerate(DIL_PAIRS):
        w_eff = widths[g]
        bv = _dist_bias(_tap_bias(rel_bias, g), d, w_eff + ts)
        rows = jnp.stack([bv[:, t + 1:t + 1 + w_eff][:, ::-1] for t in range(ts)], axis=1)
        if g == N_GROUPS - 1:
            rows = rows.reshape(H, ts, w_eff // d, d)[..., :ts].reshape(H, ts, (w_eff // d) * ts)
        tab = jnp.where(own[:, None, None, :], rows[..., None], NEG)
        cache_parts.append(tab.reshape(H * ts, -1))
        tri = jnp.stack([jnp.concatenate([bv[:, :t + 1][:, ::-1],
                                          jnp.full((H, ts - 1 - t), NEG, F32)], axis=1)
                         for t in range(ts)], axis=1)
        blk = jnp.where(same[:, None, :, None], tri[:, :, None, :], NEG).reshape(H * ts, H * ts)
        new_parts.append(jnp.concatenate([blk, jnp.full((H * ts, LANES - H * ts), NEG, F32)], axis=1))
    return jnp.concatenate(cache_parts + new_parts, axis=1)


def _window_rows(qkv, g, lo, lead):
    k0 = (SEG_BK - SEG_BQ + g) * GROUP_HEADS
    v0 = (SEG_BV - SEG_BQ + g) * GROUP_HEADS
    kv = jnp.stack([qkv[:, k0:k0 + GROUP_HEADS, lo:], qkv[:, v0:v0 + GROUP_HEADS, lo:]], axis=1)
    kv = jnp.transpose(kv, (0, 3, 1, 2, 4))
    return kv.reshape(lead + kv.shape[2:])


def kernel(x_prompt, x_sample, c_prompt, c_sample, state_gla, cache_win1, cache_win2, cache_win3,
           rel_bias, norm1_g, norm2_g, w_mod, b_mod, w_in, w_alpha, b_alpha, gla_norm_g,
           qn_g, kn_g, w_pa, w_pb, w_o, w_up, w_down):
    B, T, D = x_prompt.shape
    NS, TS, _ = x_sample.shape
    L = w_mod.shape[0]
    widths = [c.shape[2] for c in (cache_win1, cache_win2, cache_win3)]
    d2 = DIL_PAIRS[2][1]
    assert D == D_MODEL and TS == 8 and T % SUPER == 0
    assert widths[2] % d2 == 0 and TS <= d2
    caches = (cache_win1.astype(F32).reshape(L, NS, widths[0] * CACHE_ROW, LANES),
              cache_win2.astype(F32).reshape(L, NS, widths[1] * CACHE_ROW, LANES),
              cache_win3.astype(F32).reshape(L, NS, widths[2] // d2, d2 * CACHE_ROW, LANES))
    state = state_gla.astype(F32)

    mods = _mod_call(jnp.concatenate([c_prompt, c_sample], axis=0).astype(F32), w_mod, b_mod)
    mods = mods.reshape(L, B + NS, N_MOD, D)

    sizes = (GLA_DK, GLA_DK, GLA_DV, GLA_DV, GLA_RANK, B_WIDTH, B_WIDTH, B_WIDTH, D_MODEL, D_MODEL)
    offs = np.concatenate([[0], np.cumsum(sizes)])
    piece = lambda i: w_in[:, :, offs[i]:offs[i + 1]]
    w_gla = w_in[:, :, :offs[4]].astype(BF16)
    w_gate = w_in[:, :, offs[8]:].astype(BF16)
    w_b = w_in[:, :, offs[5]:offs[8]].astype(BF16)
    w_glr = jnp.pad(piece(4), ((0, 0), (0, 0), (0, LANES - GLA_RANK))).astype(BF16)
    w_al = jnp.pad(w_alpha, ((0, 0), (0, LANES - GLA_RANK), (0, 0))).astype(BF16)
    w_pa_b, w_pb_b, w_o_b = w_pa.astype(BF16), w_pb.astype(BF16), w_o.astype(BF16)
    w_up_b, w_down_b = w_up.astype(BF16), w_down.astype(BF16)

    bias_p = [_prompt_bias(rel_bias, g) for g in range(N_GROUPS)]
    bias_s = _sample_bias(rel_bias, TS, widths)

    tm = 1024
    xp = x_prompt.astype(F32)
    xs = x_sample.astype(F32)
    gla_p = []
    gla_s_all = None
    win_p = [[] for _ in range(N_GROUPS)]
    win_s = [[] for _ in range(N_GROUPS)]
    row2 = lambda v: v.reshape(1, -1)
    for l in range(L):
        mod_p, mod_s = mods[l, :B], mods[l, B:]
        pre = (row2(norm1_g[l]), w_glr, w_al, l, row2(b_alpha[l]))
        qk_gain = (row2(qn_g[l]), row2(kn_g[l]))
        gn = row2(gla_norm_g[l])

        u, la = _prenorm_call(xp, mod_p, *pre, gb=1, rb=tm)
        proj, qkv = _inproj_call(u, w_gla, w_gate, w_b, l, *qk_gain, tm=2 * tm)
        oa, s_fin = _gla_prompt_call(proj, la, gn, tb=512)
        outs = [_dil_prompt_call(qkv, bias_p[g], g, DIL_PAIRS[g][1]) for g in range(N_GROUPS)]
        xp = _merge_call(xp, mod_p, oa, proj, ([o for o, _ in outs], [s for _, s in outs]),
                         w_pa_b, w_pb_b, w_o_b, l, gb=1, rb=tm // 2)
        gla_p.append(s_fin)
        for g, (w, d) in enumerate(DIL_PAIRS):
            keep = min(w, T)
            win_p[g].append(_window_rows(qkv, g, T - keep, (B, keep)))

        u, la = _prenorm_call(xs, mod_s, *pre, gb=NS, rb=TS)
        proj, qkv = _inproj_call(u, w_gla, w_gate, w_b, l, *qk_gain, tm=NS * TS)
        proj = proj.reshape(NS, TS, MAIN_W)
        oa, gla_s_all = _gla_sample_call(proj, la, state, l, gn, sb=8, acc=gla_s_all)
        mlp_p = (xp, mod_p, row2(norm2_g[l]), w_up_b, w_down_b, l)
        row_tiles = B * (T // tm)
        tf = (D_FF * row_tiles) // NS
        if (D_FF * row_tiles) % NS == 0 and tf % 256 == 0 and D_FF % tf == 0:
            xp, ob = _mlp_call(*mlp_p, gb=1, rb=tm, tf=tf, sample=(qkv, caches, l, bias_s, NS, TS))
        else:
            xp = _mlp_call(*mlp_p, gb=1, rb=tm, tf=1024)
            ob = _dil_sample_call(qkv, caches, l, bias_s, NS, TS, sb=2)
        xs = _merge_call(xs, mod_s, oa, proj, (ob,), w_pa_b, w_pb_b, w_o_b, l, gb=NS, rb=TS)
        xs = _mlp_call(xs, mod_s, row2(norm2_g[l]), w_up_b, w_down_b, l, gb=NS, rb=TS, tf=1024)
        for g in range(N_GROUPS):
            win_s[g].append(_window_rows(qkv, g, 0, (NS, TS)))

    return (xp.astype(x_prompt.dtype), xs.astype(x_sample.dtype), jnp.stack(gla_p), gla_s_all,
            jnp.stack(win_p[0]), jnp.stack(win_s[0]), jnp.stack(win_p[1]), jnp.stack(win_s[1]),
            jnp.stack(win_p[2]), jnp.stack(win_s[2]))
```

```python
import functools

import numpy as np
import jax
import jax.numpy as jnp
from jax import lax
from jax.experimental import pallas as pl
from jax.experimental.pallas import tpu as pltpu

F32 = jnp.float32
BF16 = jnp.bfloat16

D_MODEL = 1024
GLA_HEADS = 4
GLA_DK_HEAD = 128
GLA_DV_HEAD = 256
GLA_DK = GLA_HEADS * GLA_DK_HEAD
GLA_DV = GLA_HEADS * GLA_DV_HEAD
GLA_RANK = 16
GLA_GATE_NORM = 16.0
GLA_CHUNK = 128
DIL_PAIRS = ((128, 1), (512, 4), (2048, 16))
N_GROUPS = 3
GROUP_HEADS = 4
HEAD_DIM = 128
B_WIDTH = N_GROUPS * GROUP_HEADS * HEAD_DIM
B_OUT = GROUP_HEADS * HEAD_DIM
TAPS = 129
Q_BLOCK = 128
REL_BUCKETS = 32
REL_MAX_EXACT = 16
REL_MAX_DIST = 2048
D_FF = 4 * D_MODEL
N_MOD = 6
EPS = 1e-6
NEG = -1e30

LANES = 128
CACHE_ROW = 2 * GROUP_HEADS

W_IN_SIZES = (GLA_DK, GLA_DK, GLA_DV, GLA_DV, GLA_RANK, B_WIDTH, B_WIDTH, B_WIDTH, D_MODEL, D_MODEL)
W_IN_OFFS = tuple(int(v) for v in np.concatenate([[0], np.cumsum(W_IN_SIZES)]))
W_IN_COLS = W_IN_OFFS[-1]
SEG = 512
SEG_GATE = 6
SEG_MAIN = 10
NSEG = 19
MAIN_W = SEG_MAIN * SEG
SEG_BQ, SEG_BK, SEG_BV = 10, 13, 16
N_QKV = (NSEG - SEG_MAIN) * GROUP_HEADS
SUPER = 2048

VMEM_LIMIT = 48 * 1024 * 1024


def _cparams(*sem):
    return pltpu.CompilerParams(dimension_semantics=sem, vmem_limit_bytes=VMEM_LIMIT)


def _dot(a, b):
    return jnp.dot(a, b, preferred_element_type=F32)


def _dot_nt(a, b):
    return lax.dot_general(a, b, (((1,), (1,)), ((), ())), preferred_element_type=F32)


def _dot_tn(a, b):
    return lax.dot_general(a, b, (((0,), (0,)), ((), ())), preferred_element_type=F32)


def _sigmoid(x):
    return 1.0 / (1.0 + jnp.exp(-x))


def _norm_mod(x, g, shift, scale):
    ms = jnp.mean(x * x, axis=-1, keepdims=True)
    n = x * lax.rsqrt(ms + EPS) * g
    return n * (1.0 + scale) + shift


def _rms_lanes(x, g):
    ms = jnp.mean(x * x, axis=-1, keepdims=True)
    return x * lax.rsqrt(ms + EPS) * g


def _split2(x):
    hi = x.astype(BF16)
    lo = (x - hi.astype(F32)).astype(BF16)
    return hi, lo


def _mod_kernel(c_ref, w_ref, b_ref, o_ref):
    c = c_ref[...]
    a = (c * _sigmoid(c)).astype(BF16)
    o_ref[...] = _dot(a, w_ref[...].astype(BF16)) + b_ref[...]


def _mod_call(c_all, w_mod, b_mod):
    L, D, N = w_mod.shape
    nb = c_all.shape[0]
    tn = 1024
    return pl.pallas_call(
        _mod_kernel,
        grid=(L, N // tn),
        in_specs=[pl.BlockSpec((nb, D), lambda l, n: (0, 0)),
                  pl.BlockSpec((None, D, tn), lambda l, n: (l, 0, n)),
                  pl.BlockSpec((None, 1, tn), lambda l, n: (l, 0, n))],
        out_specs=pl.BlockSpec((None, nb, tn), lambda l, n: (l, 0, n)),
        out_shape=jax.ShapeDtypeStruct((L, nb, N), F32),
        compiler_params=_cparams("arbitrary", "arbitrary"),
        name="adaln_mod",
    )(c_all, w_mod, b_mod.reshape(L, 1, N))


def _prenorm_kernel(x_ref, mod_ref, g_ref, wglr_ref, wal_ref, bal_ref, u_ref, la_ref):
    gb, rb, d = x_ref.shape
    tm = gb * rb
    mod = mod_ref[...]
    u = _norm_mod(x_ref[...], g_ref[...], mod[:, 0:1, :], mod[:, 1:2, :])
    ub = u.reshape(tm, d).astype(BF16)
    u_ref[...] = ub
    glr = _dot_nt(ub, wglr_ref[...])
    z = _dot(glr.astype(BF16), wal_ref[...]) + bal_ref[...]
    la = (jnp.minimum(z, 0.0) - jnp.log1p(jnp.exp(-jnp.abs(z)))) * (1.0 / GLA_GATE_NORM)
    la_ref[...] = la.reshape(gb, rb, GLA_DK)


def _prenorm_call(x, mod, g, w_t, wal, l, bal, gb, rb):
    G, R, D = x.shape
    tm = gb * rb
    na, nr = G // gb, R // rb
    const2 = lambda a, b: (0, 0)
    return pl.pallas_call(
        _prenorm_kernel,
        grid=(na, nr),
        in_specs=[pl.BlockSpec((gb, rb, D), lambda a, b: (a, b, 0)),
                  pl.BlockSpec((gb, N_MOD, D), lambda a, b: (a, 0, 0)),
                  pl.BlockSpec((1, D), const2),
                  pl.BlockSpec((pl.Element(LANES), pl.Element(D)),
                               lambda a, b: (pl.multiple_of(l * W_IN_COLS + W_IN_OFFS[4], 16), 0)),
                  pl.BlockSpec((None, LANES, GLA_DK), lambda a, b: (l, 0, 0)),
                  pl.BlockSpec((1, GLA_DK), const2)],
        out_specs=[pl.BlockSpec((None, tm, D), lambda a, b: (a, b, 0)),
                   pl.BlockSpec((gb, rb, GLA_DK), lambda a, b: (a, b, 0))],
        out_shape=[jax.ShapeDtypeStruct((na, nr * tm, D), BF16),
                   jax.ShapeDtypeStruct((G, R, GLA_DK), F32)],
        compiler_params=_cparams("arbitrary", "arbitrary"),
        name="prenorm",
    )(x, mod, g, w_t, wal, bal)


def _seg_col(s):
    shift = jnp.where(s < SEG_GATE, 0,
                      jnp.where(s < SEG_MAIN, W_IN_OFFS[8] - SEG * SEG_GATE, W_IN_OFFS[5] - SEG * SEG_MAIN))
    return SEG * s + shift


def _inproj_kernel(u_ref, w_ref, qn_ref, kn_ref, proj_ref, qkv_ref):
    s = pl.program_id(2)
    tm = u_ref.shape[0]
    rc = min(tm, 512)

    @pl.when(s < SEG_MAIN)
    def _():
        proj_ref[...] = _dot_nt(u_ref[...], w_ref[...])

    def to_heads(r0, rows, val):
        for h in range(GROUP_HEADS):
            qkv_ref[h, r0:r0 + rows, :] = val[:, h * HEAD_DIM:(h + 1) * HEAD_DIM]

    def normed(gain_ref):
        g = gain_ref[...]
        for c in range(tm // rc):
            r = _dot_nt(u_ref[c * rc:(c + 1) * rc, :], w_ref[...])
            to_heads(c * rc, rc, jnp.concatenate(
                [_rms_lanes(r[:, h * HEAD_DIM:(h + 1) * HEAD_DIM], g) for h in range(GROUP_HEADS)],
                axis=1))

    @pl.when((s >= SEG_BQ) & (s < SEG_BK))
    def _():
        normed(qn_ref)

    @pl.when((s >= SEG_BK) & (s < SEG_BV))
    def _():
        normed(kn_ref)

    @pl.when(s >= SEG_BV)
    def _():
        to_heads(0, tm, _dot_nt(u_ref[...], w_ref[...]))


def _inproj_call(u, w_t, l, qn, kn, tm):
    A, N, D = u.shape
    const2 = lambda a, b, s: (0, 0)
    return pl.pallas_call(
        _inproj_kernel,
        grid=(A, N // tm, NSEG),
        in_specs=[pl.BlockSpec((None, tm, D), lambda a, b, s: (a, b, 0)),
                  pl.BlockSpec((pl.Element(SEG), pl.Element(D)),
                               lambda a, b, s: (pl.multiple_of(l * W_IN_COLS + _seg_col(s), 16), 0)),
                  pl.BlockSpec((1, HEAD_DIM), const2),
                  pl.BlockSpec((1, HEAD_DIM), const2)],
        out_specs=[pl.BlockSpec((None, tm, SEG), lambda a, b, s: (a, b, jnp.minimum(s, SEG_MAIN - 1))),
                   pl.BlockSpec((None, GROUP_HEADS, tm, HEAD_DIM),
                                lambda a, b, s: (a, jnp.clip(s - SEG_BQ, 0, NSEG - SEG_BQ - 1), b, 0))],
        out_shape=[jax.ShapeDtypeStruct((A, N, MAIN_W), F32),
                   jax.ShapeDtypeStruct((A, N_QKV, N, HEAD_DIM), F32)],
        compiler_params=_cparams("arbitrary", "arbitrary", "arbitrary"),
        name="in_proj",
    )(u, w_t, qn, kn)


def _gla_unit(q, k, v, la, S, tril, lmat, ones_c):
    dk = la.shape[1]
    la2 = jnp.concatenate(_split2(la), axis=1)
    b2 = _dot(lmat, la2)
    b = b2[:, :dk] + b2[:, dk:]
    half = b.shape[0] // 2
    b_mid = b[half - 1:half, :]
    qs = q * (GLA_DK_HEAD ** -0.5)
    qe = (qs * jnp.exp(b)).astype(BF16)
    qm = (qs * jnp.exp(b - b_mid)).astype(BF16)
    km = (k * jnp.exp(b_mid - b)).astype(BF16)
    vb = v.astype(BF16)
    sc = jnp.where(tril, _dot_nt(qm, km), 0.0)
    o = _dot(jnp.concatenate([qe, sc.astype(BF16)], axis=1),
             jnp.concatenate([S.astype(BF16), vb], axis=0))
    b_last = b[b.shape[0] - 1:, :]
    kd = (k * jnp.exp(b_last - b)).astype(BF16)
    d2 = _dot_tn(la2, ones_c)
    dec = jnp.exp(d2[:dk] + d2[dk:])
    s_new = S * jnp.concatenate([dec, dec], axis=1) + _dot_tn(kd, vb)
    return o, s_new


def _gla_prompt_kernel(q_ref, k_ref, v_ref, la_ref, gn_ref, o_ref, sfin_ref, s_scr):
    j = pl.program_id(0)
    nb, tb, _ = q_ref.shape
    C = GLA_CHUNK

    @pl.when(j == 0)
    def _():
        s_scr[...] = jnp.zeros_like(s_scr)

    row = lax.broadcasted_iota(jnp.int32, (C, C), 0)
    col = lax.broadcasted_iota(jnp.int32, (C, C), 1)
    tril = col <= row
    lmat = tril.astype(BF16)
    ones_c = jnp.ones((C, LANES), BF16)
    gn = gn_ref[...]
    for b in range(nb):
        for h in range(GLA_HEADS):
            ksl = slice(h * GLA_DK_HEAD, (h + 1) * GLA_DK_HEAD)
            vsl = slice(h * GLA_DV_HEAD, (h + 1) * GLA_DV_HEAD)
            S = s_scr[b, h]
            for c in range(tb // C):
                rows = slice(c * C, (c + 1) * C)
                o, S = _gla_unit(q_ref[b, rows, ksl], k_ref[b, rows, ksl], v_ref[b, rows, vsl],
                                 la_ref[b, rows, ksl], S, tril, lmat, ones_c)
                o_ref[b, rows, vsl] = _rms_lanes(o, gn)
            s_scr[b, h] = S

    @pl.when(j == pl.num_programs(0) - 1)
    def _():
        sfin_ref[...] = s_scr[...]


def _gla_prompt_call(proj, la, gn, tb):
    B, T, _ = proj.shape
    st = (B, GLA_HEADS, GLA_DK_HEAD, GLA_DV_HEAD)
    return pl.pallas_call(
        _gla_prompt_kernel,
        grid=(T // tb,),
        in_specs=[pl.BlockSpec((B, tb, GLA_DK), lambda j: (0, j, 0)),
                  pl.BlockSpec((B, tb, GLA_DK), lambda j: (0, j, 1)),
                  pl.BlockSpec((B, tb, GLA_DV), lambda j: (0, j, 1)),
                  pl.BlockSpec((B, tb, GLA_DK), lambda j: (0, j, 0)),
                  pl.BlockSpec((1, GLA_DV_HEAD), lambda j: (0, 0))],
        out_specs=[pl.BlockSpec((B, tb, GLA_DV), lambda j: (0, j, 0)),
                   pl.BlockSpec(st, lambda j: (0, 0, 0, 0))],
        out_shape=[jax.ShapeDtypeStruct((B, T, GLA_DV), F32), jax.ShapeDtypeStruct(st, F32)],
        scratch_shapes=[pltpu.VMEM(st, F32)],
        compiler_params=_cparams("arbitrary"),
        name="gla_prompt",
    )(proj, proj, proj, la, gn)


def _gla_sample_kernel(q_ref, k_ref, v_ref, la_ref, sin_ref, gn_ref, *rest):
    o_ref, sout_ref = rest[-2:]
    sb, ts, _ = q_ref.shape
    R = sb * ts
    row = lax.broadcasted_iota(jnp.int32, (R, R), 0)
    col = lax.broadcasted_iota(jnp.int32, (R, R), 1)
    shift = ts.bit_length() - 1
    same = jnp.right_shift(row, shift) == jnp.right_shift(col, shift)
    tril = same & (col <= row)
    lmat = tril.astype(BF16)
    bones = same.astype(BF16)
    ones_t = jnp.ones((ts, LANES), F32)
    gn = gn_ref[...]
    for h in range(GLA_HEADS):
        ksl = slice(h * GLA_DK_HEAD, (h + 1) * GLA_DK_HEAD)
        vsl = slice(h * GLA_DV_HEAD, (h + 1) * GLA_DV_HEAD)
        q = q_ref[:, :, ksl].reshape(R, GLA_DK_HEAD) * (GLA_DK_HEAD ** -0.5)
        k = k_ref[:, :, ksl].reshape(R, GLA_DK_HEAD)
        v = v_ref[:, :, vsl].reshape(R, GLA_DV_HEAD)
        la = la_ref[:, :, ksl].reshape(R, GLA_DK_HEAD)
        la_hi, la_lo = _split2(la)
        b = _dot(lmat, la_hi) + _dot(lmat, la_lo)
        btot = _dot(bones, la_hi) + _dot(bones, la_lo)
        qe = q * jnp.exp(b)
        ke = (k * jnp.exp(-b)).astype(BF16)
        kd = k * jnp.exp(btot - b)
        vb = v.astype(BF16)
        sc = jnp.where(tril, _dot_nt(qe.astype(BF16), ke), 0.0)
        o_intra = _dot(sc.astype(BF16), vb)
        la_hi32, la_lo32 = la_hi.astype(F32), la_lo.astype(F32)
        for i in range(sb):
            rs = slice(i * ts, (i + 1) * ts)
            S = sin_ref[i, h]
            o = _dot(qe[rs].astype(BF16), S.astype(BF16)) + o_intra[rs]
            kv = _dot_tn(kd[rs].astype(BF16), v[rs].astype(BF16))
            dec = jnp.exp(_dot_tn(la_hi32[rs].astype(BF16), ones_t.astype(BF16))
                          + _dot_tn(la_lo32[rs].astype(BF16), ones_t.astype(BF16)))
            sout_ref[i, h] = S * jnp.concatenate([dec, dec], axis=1) + kv
            o_ref[i, :, vsl] = _rms_lanes(o, gn)


def _gla_sample_call(proj, la, state, l, gn, sb, acc):
    NS, TS, _ = proj.shape
    st_shape = (None, sb, GLA_HEADS, GLA_DK_HEAD, GLA_DV_HEAD)
    st_spec = pl.BlockSpec(st_shape, lambda i: (l, i, 0, 0, 0))
    in_specs = [pl.BlockSpec((sb, TS, GLA_DK), lambda i: (i, 0, 0)),
                pl.BlockSpec((sb, TS, GLA_DK), lambda i: (i, 0, 1)),
                pl.BlockSpec((sb, TS, GLA_DV), lambda i: (i, 0, 1)),
                pl.BlockSpec((sb, TS, GLA_DK), lambda i: (i, 0, 0)),
                st_spec,
                pl.BlockSpec((1, GLA_DV_HEAD), lambda i: (0, 0))]
    args = [proj, proj, proj, la, state, gn]
    aliases = {}
    if acc is not None:
        in_specs.append(pl.BlockSpec(memory_space=pl.ANY))
        args.append(acc)
        aliases = {len(args) - 1: 1}
    return pl.pallas_call(
        _gla_sample_kernel,
        grid=(NS // sb,),
        in_specs=in_specs,
        out_specs=[pl.BlockSpec((sb, TS, GLA_DV), lambda i: (i, 0, 0)), st_spec],
        out_shape=[jax.ShapeDtypeStruct((NS, TS, GLA_DV), F32),
                   jax.ShapeDtypeStruct(state.shape, F32)],
        input_output_aliases=aliases,
        compiler_params=_cparams("arbitrary"),
        name="gla_sample",
    )(*args)


def _dil_prompt_kernel(q_ref, k_ref, v_ref, bias_ref, o_ref, lse_ref, kc_scr, vc_scr, *, d):
    n = pl.program_id(1)
    h = pl.program_id(2)
    nsub = SUPER // (Q_BLOCK * d)
    scale = HEAD_DIM ** -0.5

    @pl.when(n == 0)
    def _():
        kc_scr[h] = jnp.zeros(kc_scr.shape[1:], BF16)
        vc_scr[h] = jnp.zeros(vc_scr.shape[1:], BF16)

    bias = bias_ref[...]
    col = lax.broadcasted_iota(jnp.int32, bias.shape, 1)
    bias_first = bias + jnp.where((col < Q_BLOCK) & (n == 0), NEG, 0.0)
    ones = jnp.ones((Q_BLOCK, LANES), BF16)
    for p in range(d):
        cs = slice(p * Q_BLOCK, (p + 1) * Q_BLOCK)
        kp = kc_scr[h, cs, :]
        vp = vc_scr[h, cs, :]
        for i in range(nsub):
            start = p + d * Q_BLOCK * i
            rows = pl.ds(start, Q_BLOCK, stride=d) if d > 1 else pl.ds(start, Q_BLOCK)
            q = q_ref[rows, :].astype(BF16)
            kc = k_ref[rows, :].astype(BF16)
            vc = v_ref[rows, :].astype(BF16)
            lg = (_dot_nt(q, jnp.concatenate([kp, kc], axis=0)) * scale
                  + (bias_first if i == 0 else bias))
            m = jnp.max(lg, axis=-1, keepdims=True)
            pr = jnp.exp(lg - m).astype(BF16)
            v2 = jnp.concatenate([jnp.concatenate([vp, ones], axis=1),
                                  jnp.concatenate([vc, ones], axis=1)], axis=0)
            acc = _dot(pr, v2)
            s = acc[:, HEAD_DIM:]
            o_ref[rows, :] = acc[:, :HEAD_DIM] / s
            lse_ref[rows, :] = m + jnp.log(s)
            kp, vp = kc, vc
        kc_scr[h, cs, :] = kp
        vc_scr[h, cs, :] = vp


def _dil_prompt_call(qkv, bias_g, g, d):
    B, _, T, _ = qkv.shape
    plane = lambda seg: (seg - SEG_BQ + g) * GROUP_HEADS

    def spec(seg):
        base = plane(seg)
        return pl.BlockSpec((None, None, SUPER, HEAD_DIM), lambda b, n, h: (b, base + h, n, 0))

    out_spec = pl.BlockSpec((None, None, SUPER, HEAD_DIM), lambda b, n, h: (b, h, n, 0))
    out_sds = jax.ShapeDtypeStruct((B, GROUP_HEADS, T, HEAD_DIM), F32)
    carry = pltpu.VMEM((GROUP_HEADS, d * Q_BLOCK, HEAD_DIM), BF16)
    return pl.pallas_call(
        functools.partial(_dil_prompt_kernel, d=d),
        grid=(B, T // SUPER, GROUP_HEADS),
        in_specs=[spec(SEG_BQ), spec(SEG_BK), spec(SEG_BV),
                  pl.BlockSpec((None, Q_BLOCK, 2 * Q_BLOCK), lambda b, n, h: (h, 0, 0))],
        out_specs=[out_spec, out_spec],
        out_shape=[out_sds, out_sds],
        scratch_shapes=[carry, carry],
        compiler_params=_cparams("arbitrary", "arbitrary", "arbitrary"),
        name="dilated_prompt_g%d" % g,
    )(qkv, qkv, qkv, bias_g)


def _dil_sample_kernel(qkv_ref, c0_ref, c1_ref, c2_ref, bias_ref, o_ref):
    sb, ts, _ = o_ref.shape
    scale = HEAD_DIM ** -0.5
    qrows = GROUP_HEADS * ts
    n2 = c2_ref.shape[1] * c2_ref.shape[2]
    zpad = jnp.zeros((LANES - qrows, HEAD_DIM), F32)
    for j in range(sb):
        xs = (c0_ref[j].astype(BF16), c1_ref[j].astype(BF16),
              c2_ref[j].reshape(n2, LANES).astype(BF16))

        def planes(seg):
            return jnp.concatenate([qkv_ref[seg * GROUP_HEADS + h, j * ts:(j + 1) * ts, :]
                                    for h in range(GROUP_HEADS)], axis=0)

        qs = [planes(g).astype(BF16) for g in range(N_GROUPS)]
        parts = []
        off = 0
        for g in range(N_GROUPS):
            w = xs[g].shape[0]
            parts.append((_dot_nt(qs[g], xs[g]) * scale + bias_ref[:, off:off + w], xs[g], True))
            off += w
        for g in range(N_GROUPS):
            kn = jnp.concatenate([planes(N_GROUPS + g), zpad], axis=0).astype(BF16)
            vn = jnp.concatenate([planes(2 * N_GROUPS + g), zpad], axis=0).astype(BF16)
            parts.append((_dot_nt(qs[g], kn) * scale + bias_ref[:, off:off + LANES], vn, False))
            off += LANES
        m = parts[0][0].max(axis=-1, keepdims=True)
        for lg, _, _ in parts[1:]:
            m = jnp.maximum(m, lg.max(axis=-1, keepdims=True))
        s = jnp.zeros((qrows, 1), F32)
        acc = jnp.zeros((qrows, HEAD_DIM), F32)
        for lg, vv, cached in parts:
            p = jnp.exp(lg - m)
            s = s + jnp.sum(p, axis=-1, keepdims=True)
            if cached:
                p = pltpu.roll(p, GROUP_HEADS, axis=1)
            acc = acc + _dot(p.astype(BF16), vv)
        res = acc / s
        for h in range(GROUP_HEADS):
            o_ref[j, :, h * HEAD_DIM:(h + 1) * HEAD_DIM] = res[h * ts:(h + 1) * ts]


def _dil_sample_specs(caches, l, bias_s, ts, sb):
    c0, c1, c2 = caches
    in_specs = [pl.BlockSpec((None, N_QKV, sb * ts, HEAD_DIM), lambda i: (0, 0, i, 0)),
                pl.BlockSpec((None, sb) + c0.shape[2:], lambda i: (l, i, 0, 0)),
                pl.BlockSpec((None, sb) + c1.shape[2:], lambda i: (l, i, 0, 0)),
                pl.BlockSpec((None, sb, c2.shape[2], ts * CACHE_ROW, LANES),
                             lambda i: (l, i, 0, 0, 0)),
                pl.BlockSpec(bias_s.shape, lambda i: (0, 0))]
    return in_specs, pl.BlockSpec((sb, ts, B_OUT), lambda i: (i, 0, 0))


def _dil_sample_call(qkv, caches, l, bias_s, ns, ts, sb):
    in_specs, out_spec = _dil_sample_specs(caches, l, bias_s, ts, sb)
    return pl.pallas_call(
        _dil_sample_kernel,
        grid=(ns // sb,),
        in_specs=in_specs,
        out_specs=out_spec,
        out_shape=jax.ShapeDtypeStruct((ns, ts, B_OUT), F32),
        compiler_params=_cparams("arbitrary"),
        name="dilated_sample",
    )(qkv, *caches, bias_s)


def _merge_kernel(*refs, combine):
    if combine:
        (x_ref, mod_ref, oa_ref, gr_ref, ga_ref, gb_ref, o0_ref, o1_ref, o2_ref,
         l0_ref, l1_ref, l2_ref, wpa_ref, wpb_ref, wo_ref, out_ref) = refs
    else:
        (x_ref, mod_ref, oa_ref, gr_ref, ga_ref, gb_ref, ob_ref,
         wpa_ref, wpb_ref, wo_ref, out_ref) = refs
    gb, rb, d = x_ref.shape
    tm = gb * rb
    if combine:
        o_refs = (o0_ref, o1_ref, o2_ref)
        l_refs = (l0_ref, l1_ref, l2_ref)
        parts = []
        for h in range(GROUP_HEADS):
            ls = [l[h] for l in l_refs]
            m = jnp.maximum(jnp.maximum(ls[0], ls[1]), ls[2])
            es = [jnp.exp(l - m) for l in ls]
            den = es[0] + es[1] + es[2]
            parts.append(sum((es[g] / den) * o_refs[g][h] for g in range(N_GROUPS)))
        ob = jnp.concatenate(parts, axis=1)
    else:
        ob = ob_ref[...].reshape(tm, B_OUT)
    gr = gr_ref[...].reshape(tm, GLA_DV)
    oa = oa_ref[...].reshape(tm, GLA_DV) * (gr * _sigmoid(gr))
    m1 = _dot(oa.astype(BF16), wpa_ref[...])
    m2 = _dot(ob.astype(BF16), wpb_ref[...])
    merged = (_sigmoid(ga_ref[...].reshape(tm, d)) * m1
              + _sigmoid(gb_ref[...].reshape(tm, d)) * m2)
    y = _dot(merged.astype(BF16), wo_ref[...]).reshape(gb, rb, d)
    out_ref[...] = x_ref[...] + mod_ref[...][:, 2:3, :] * y


def _merge_call(x, mod, oa, proj, ob_parts, wpa, wpb, wo, l, gb, rb):
    G, R, D = x.shape
    combine = len(ob_parts) > 1
    tm = gb * rb
    tok = lambda w, c: pl.BlockSpec((gb, rb, w), lambda a, b: (a, b, c))
    const = lambda shp: pl.BlockSpec((None,) + shp, lambda a, b: (l, 0, 0))
    in_specs = [tok(D, 0), pl.BlockSpec((gb, N_MOD, D), lambda a, b: (a, 0, 0)),
                tok(GLA_DV, 0), tok(GLA_DV, 2), tok(D, 3), tok(D, 4)]
    args = [x, mod, oa, proj, proj, proj]
    if combine:
        os_, ls_ = ob_parts
        head_major = pl.BlockSpec((None, GROUP_HEADS, tm, HEAD_DIM), lambda a, b: (a, 0, b, 0))
        in_specs += [head_major] * (2 * N_GROUPS)
        args += list(os_) + list(ls_)
    else:
        in_specs += [tok(B_OUT, 0)]
        args += list(ob_parts)
    in_specs += [const((GLA_DV, D)), const((B_OUT, D)), const((D, D))]
    args += [wpa, wpb, wo]
    return pl.pallas_call(
        functools.partial(_merge_kernel, combine=combine),
        grid=(G // gb, R // rb),
        in_specs=in_specs,
        out_specs=tok(D, 0),
        out_shape=jax.ShapeDtypeStruct((G, R, D), F32),
        compiler_params=_cparams("arbitrary", "arbitrary"),
        name="merge_combine" if combine else "merge",
    )(*args)


def _mlp_kernel(x_ref, mod_ref, g_ref, wu_ref, wd_ref, out_ref, u_scr, acc_scr):
    f = pl.program_id(2)
    gb, rb, d = x_ref.shape
    tm = gb * rb

    @pl.when(f == 0)
    def _():
        mod = mod_ref[...]
        u = _norm_mod(x_ref[...], g_ref[...], mod[:, 3:4, :], mod[:, 4:5, :])
        u_scr[...] = u.reshape(tm, d).astype(BF16)
        acc_scr[...] = jnp.zeros_like(acc_scr)

    hid = jnp.maximum(_dot(u_scr[...], wu_ref[...]), 0.0)
    acc_scr[...] += _dot((hid * hid).astype(BF16), wd_ref[...])

    @pl.when(f == pl.num_programs(2) - 1)
    def _():
        out_ref[...] = x_ref[...] + mod_ref[...][:, 5:6, :] * acc_scr[...].reshape(gb, rb, d)


def _mlp_call(x, mod, g, wu, wd, l, gb, rb, tf):
    G, R, D = x.shape
    F = wu.shape[2]
    tm = gb * rb
    return pl.pallas_call(
        _mlp_kernel,
        grid=(G // gb, R // rb, F // tf),
        in_specs=[pl.BlockSpec((gb, rb, D), lambda a, b, f: (a, b, 0)),
                  pl.BlockSpec((gb, N_MOD, D), lambda a, b, f: (a, 0, 0)),
                  pl.BlockSpec((1, D), lambda a, b, f: (0, 0)),
                  pl.BlockSpec((None, D, tf), lambda a, b, f: (l, 0, f)),
                  pl.BlockSpec((None, tf, D), lambda a, b, f: (l, f, 0))],
        out_specs=pl.BlockSpec((gb, rb, D), lambda a, b, f: (a, b, 0)),
        out_shape=jax.ShapeDtypeStruct((G, R, D), F32),
        scratch_shapes=[pltpu.VMEM((tm, D), BF16), pltpu.VMEM((tm, D), F32)],
        compiler_params=_cparams("arbitrary", "arbitrary", "arbitrary"),
        name="mlp",
    )(x, mod, g, wu, wd)


def _t5_bucket(dist):
    dist = np.asarray(dist)
    large = REL_MAX_EXACT + (np.log(np.maximum(dist, 1) / REL_MAX_EXACT)
                             / np.log(REL_MAX_DIST / REL_MAX_EXACT)
                             * (REL_BUCKETS - REL_MAX_EXACT)).astype(np.int32)
    large = np.minimum(large, REL_BUCKETS - 1)
    return np.where(dist < REL_MAX_EXACT, dist, large).astype(np.int32)


def _tap_bias(rel_bias, g):
    d = DIL_PAIRS[g][1]
    idx = _t5_bucket(d * np.arange(TAPS))
    return rel_bias[idx][:, g * GROUP_HEADS:(g + 1) * GROUP_HEADS].astype(F32)


def _prompt_bias(rel_bias, g):
    tb = _tap_bias(rel_bias, g)
    w = jnp.concatenate([tb[::-1].T, jnp.full((GROUP_HEADS, Q_BLOCK), NEG, F32)], axis=1)
    flat = jnp.tile(w, (1, Q_BLOCK))[:, :Q_BLOCK * 2 * Q_BLOCK]
    return flat.reshape(GROUP_HEADS, Q_BLOCK, 2 * Q_BLOCK)


def _dist_bias(tb, d, n):
    h = tb.shape[1]
    v = jnp.concatenate([tb[:, None, :], jnp.full((TAPS, d - 1, h), NEG, F32)], axis=1)
    v = v.reshape(TAPS * d, h)
    if n > TAPS * d:
        v = jnp.concatenate([v, jnp.full((n - TAPS * d, h), NEG, F32)], axis=0)
    return v[:n].T


def _sample_bias(rel_bias, ts, widths):
    H = GROUP_HEADS
    heads = np.arange(H)
    own = jnp.asarray(np.arange(CACHE_ROW)[None, :] == heads[:, None])
    same = jnp.asarray(heads[:, None] == heads[None, :])
    cache_parts, new_parts = [], []
    for g, (w, d) in enumerate(DIL_PAIRS):
        w_eff = widths[g]
        bv = _dist_bias(_tap_bias(rel_bias, g), d, w_eff + ts)
        rows = jnp.stack([bv[:, t + 1:t + 1 + w_eff][:, ::-1] for t in range(ts)], axis=1)
        if g == N_GROUPS - 1:
            rows = rows.reshape(H, ts, w_eff // d, d)[..., :ts].reshape(H, ts, (w_eff // d) * ts)
        tab = jnp.where(own[:, None, None, :], rows[..., None], NEG)
        cache_parts.append(tab.reshape(H * ts, -1))
        tri = jnp.stack([jnp.concatenate([bv[:, :t + 1][:, ::-1],
                                          jnp.full((H, ts - 1 - t), NEG, F32)], axis=1)
                         for t in range(ts)], axis=1)
        blk = jnp.where(same[:, None, :, None], tri[:, :, None, :], NEG).reshape(H * ts, H * ts)
        new_parts.append(jnp.concatenate([blk, jnp.full((H * ts, LANES - H * ts), NEG, F32)], axis=1))
    return jnp.concatenate(cache_parts + new_parts, axis=1)


def _window_rows(qkv, g, lo, lead):
    k0 = (SEG_BK - SEG_BQ + g) * GROUP_HEADS
    v0 = (SEG_BV - SEG_BQ + g) * GROUP_HEADS
    kv = jnp.stack([qkv[:, k0:k0 + GROUP_HEADS, lo:], qkv[:, v0:v0 + GROUP_HEADS, lo:]], axis=1)
    kv = jnp.transpose(kv, (0, 3, 1, 2, 4))
    return kv.reshape(lead + kv.shape[2:])


def kernel(x_prompt, x_sample, c_prompt, c_sample, state_gla, cache_win1, cache_win2, cache_win3,
           rel_bias, norm1_g, norm2_g, w_mod, b_mod, w_in, w_alpha, b_alpha, gla_norm_g,
           qn_g, kn_g, w_pa, w_pb, w_o, w_up, w_down):
    B, T, D = x_prompt.shape
    NS, TS, _ = x_sample.shape
    L = w_mod.shape[0]
    widths = [c.shape[2] for c in (cache_win1, cache_win2, cache_win3)]
    d2 = DIL_PAIRS[2][1]
    assert D == D_MODEL and TS == 8 and T % SUPER == 0
    assert widths[2] % d2 == 0 and TS <= d2
    caches = (cache_win1.astype(F32).reshape(L, NS, widths[0] * CACHE_ROW, LANES),
              cache_win2.astype(F32).reshape(L, NS, widths[1] * CACHE_ROW, LANES),
              cache_win3.astype(F32).reshape(L, NS, widths[2] // d2, d2 * CACHE_ROW, LANES))
    state = state_gla.astype(F32)

    mods = _mod_call(jnp.concatenate([c_prompt, c_sample], axis=0).astype(F32), w_mod, b_mod)
    mods = mods.reshape(L, B + NS, N_MOD, D)

    assert w_in.shape[2] == W_IN_COLS
    w_t = jnp.swapaxes(w_in, 1, 2).astype(BF16).reshape(L * W_IN_COLS, D)
    w_al = jnp.pad(w_alpha, ((0, 0), (0, LANES - GLA_RANK), (0, 0))).astype(BF16)
    w_pa_b, w_pb_b, w_o_b = w_pa.astype(BF16), w_pb.astype(BF16), w_o.astype(BF16)
    w_up_b, w_down_b = w_up.astype(BF16), w_down.astype(BF16)

    bias_p = [_prompt_bias(rel_bias, g) for g in range(N_GROUPS)]
    bias_s = _sample_bias(rel_bias, TS, widths)

    tm = 1024
    xp = x_prompt.astype(F32)
    xs = x_sample.astype(F32)
    gla_p = []
    gla_s_all = None
    win_p = [[] for _ in range(N_GROUPS)]
    win_s = [[] for _ in range(N_GROUPS)]
    row2 = lambda v: v.reshape(1, -1)
    for l in range(L):
        mod_p, mod_s = mods[l, :B], mods[l, B:]
        pre = (row2(norm1_g[l]), w_t, w_al, l, row2(b_alpha[l]))
        qk_gain = (row2(qn_g[l]), row2(kn_g[l]))
        gn = row2(gla_norm_g[l])

        u, la = _prenorm_call(xp, mod_p, *pre, gb=1, rb=tm)
        proj, qkv = _inproj_call(u, w_t, l, *qk_gain, tm=2 * tm)
        oa, s_fin = _gla_prompt_call(proj, la, gn, tb=512)
        outs = [_dil_prompt_call(qkv, bias_p[g], g, DIL_PAIRS[g][1]) for g in range(N_GROUPS)]
        xp = _merge_call(xp, mod_p, oa, proj, ([o for o, _ in outs], [s for _, s in outs]),
                         w_pa_b, w_pb_b, w_o_b, l, gb=1, rb=tm // 2)
        xp = _mlp_call(xp, mod_p, row2(norm2_g[l]), w_up_b, w_down_b, l, gb=1, rb=tm, tf=1024)
        gla_p.append(s_fin)
        for g, (w, d) in enumerate(DIL_PAIRS):
            keep = min(w, T)
            win_p[g].append(_window_rows(qkv, g, T - keep, (B, keep)))

        u, la = _prenorm_call(xs, mod_s, *pre, gb=NS, rb=TS)
        proj, qkv = _inproj_call(u, w_t, l, *qk_gain, tm=NS * TS)
        proj = proj.reshape(NS, TS, MAIN_W)
        oa, gla_s_all = _gla_sample_call(proj, la, state, l, gn, sb=8, acc=gla_s_all)
        ob = _dil_sample_call(qkv, caches, l, bias_s, NS, TS, sb=2)
        xs = _merge_call(xs, mod_s, oa, proj, (ob,), w_pa_b, w_pb_b, w_o_b, l, gb=NS, rb=TS)
        xs = _mlp_call(xs, mod_s, row2(norm2_g[l]), w_up_b, w_down_b, l, gb=NS, rb=TS, tf=1024)
        for g in range(N_GROUPS):
            win_s[g].append(_window_rows(qkv, g, 0, (NS, TS)))

    return (xp.astype(x_prompt.dtype), xs.astype(x_sample.dtype), jnp.stack(gla_p), gla_s_all,
            jnp.stack(win_p[0]), jnp.stack(win_s[0]), jnp.stack(win_p[1]), jnp.stack(win_s[1]),
            jnp.stack(win_p[2]), jnp.stack(win_s[2]))
```

```python
import functools

import numpy as np
import jax
import jax.numpy as jnp
from jax import lax
from jax.experimental import pallas as pl
from jax.experimental.pallas import tpu as pltpu

F32 = jnp.float32
BF16 = jnp.bfloat16

D_MODEL = 1024
GLA_HEADS = 4
GLA_DK_HEAD = 128
GLA_DV_HEAD = 256
GLA_DK = GLA_HEADS * GLA_DK_HEAD
GLA_DV = GLA_HEADS * GLA_DV_HEAD
GLA_RANK = 16
GLA_GATE_NORM = 16.0
GLA_CHUNK = 128
DIL_PAIRS = ((128, 1), (512, 4), (2048, 16))
N_GROUPS = 3
GROUP_HEADS = 4
HEAD_DIM = 128
B_WIDTH = N_GROUPS * GROUP_HEADS * HEAD_DIM
B_OUT = GROUP_HEADS * HEAD_DIM
TAPS = 129
Q_BLOCK = 128
REL_BUCKETS = 32
REL_MAX_EXACT = 16
REL_MAX_DIST = 2048
D_FF = 4 * D_MODEL
N_MOD = 6
EPS = 1e-6
NEG = -1e30

LANES = 128
CACHE_ROW = 2 * GROUP_HEADS

W_IN_SIZES = (GLA_DK, GLA_DK, GLA_DV, GLA_DV, GLA_RANK, B_WIDTH, B_WIDTH, B_WIDTH, D_MODEL, D_MODEL)
W_IN_OFFS = tuple(int(v) for v in np.concatenate([[0], np.cumsum(W_IN_SIZES)]))
W_IN_COLS = W_IN_OFFS[-1]
SEG = 512
SEG_GATE = 6
SEG_MAIN = 10
NSEG = 19
MAIN_W = SEG_MAIN * SEG
SEG_BQ, SEG_BK, SEG_BV = 10, 13, 16
N_QKV = (NSEG - SEG_MAIN) * GROUP_HEADS
SUPER = 2048
LSE_LANES = LANES // GROUP_HEADS

VMEM_LIMIT = 48 * 1024 * 1024


def _cparams(*sem):
    return pltpu.CompilerParams(dimension_semantics=sem, vmem_limit_bytes=VMEM_LIMIT)


def _dot(a, b):
    return jnp.dot(a, b, preferred_element_type=F32)


def _dot_nt(a, b):
    return lax.dot_general(a, b, (((1,), (1,)), ((), ())), preferred_element_type=F32)


def _dot_tn(a, b):
    return lax.dot_general(a, b, (((0,), (0,)), ((), ())), preferred_element_type=F32)


def _sigmoid(x):
    return 1.0 / (1.0 + jnp.exp(-x))


def _norm_mod(x, g, shift, scale):
    ms = jnp.mean(x * x, axis=-1, keepdims=True)
    n = x * lax.rsqrt(ms + EPS) * g
    return n * (1.0 + scale) + shift


def _rms_lanes(x, g):
    ms = jnp.mean(x * x, axis=-1, keepdims=True)
    return x * lax.rsqrt(ms + EPS) * g


def _split2(x):
    hi = x.astype(BF16)
    lo = (x - hi.astype(F32)).astype(BF16)
    return hi, lo


def _mod_kernel(c_ref, w_ref, b_ref, o_ref):
    c = c_ref[...]
    a = (c * _sigmoid(c)).astype(BF16)
    o_ref[...] = _dot(a, w_ref[...].astype(BF16)) + b_ref[...]


def _mod_call(c_all, w_mod, b_mod):
    L, D, N = w_mod.shape
    nb = c_all.shape[0]
    tn = 1024
    return pl.pallas_call(
        _mod_kernel,
        grid=(L, N // tn),
        in_specs=[pl.BlockSpec((nb, D), lambda l, n: (0, 0)),
                  pl.BlockSpec((None, D, tn), lambda l, n: (l, 0, n)),
                  pl.BlockSpec((None, 1, tn), lambda l, n: (l, 0, n))],
        out_specs=pl.BlockSpec((None, nb, tn), lambda l, n: (l, 0, n)),
        out_shape=jax.ShapeDtypeStruct((L, nb, N), F32),
        compiler_params=_cparams("arbitrary", "arbitrary"),
        name="adaln_mod",
    )(c_all, w_mod, b_mod.reshape(L, 1, N))


def _prenorm_kernel(x_ref, mod_ref, g_ref, wglr_ref, wal_ref, bal_ref, u_ref, la_ref):
    gb, rb, d = x_ref.shape
    tm = gb * rb
    mod = mod_ref[...]
    u = _norm_mod(x_ref[...], g_ref[...], mod[:, 0:1, :], mod[:, 1:2, :])
    ub = u.reshape(tm, d).astype(BF16)
    u_ref[...] = ub
    glr = _dot_nt(ub, wglr_ref[...])
    z = _dot(glr.astype(BF16), wal_ref[...]) + bal_ref[...]
    la = (jnp.minimum(z, 0.0) - jnp.log1p(jnp.exp(-jnp.abs(z)))) * (1.0 / GLA_GATE_NORM)
    la_ref[...] = la.reshape(gb, rb, GLA_DK)


def _prenorm_call(x, mod, g, w_t, wal, l, bal, gb, rb):
    G, R, D = x.shape
    tm = gb * rb
    na, nr = G // gb, R // rb
    const2 = lambda a, b: (0, 0)
    return pl.pallas_call(
        _prenorm_kernel,
        grid=(na, nr),
        in_specs=[pl.BlockSpec((gb, rb, D), lambda a, b: (a, b, 0)),
                  pl.BlockSpec((gb, N_MOD, D), lambda a, b: (a, 0, 0)),
                  pl.BlockSpec((1, D), const2),
                  pl.BlockSpec((pl.Element(LANES), pl.Element(D)),
                               lambda a, b: (pl.multiple_of(l * W_IN_COLS + W_IN_OFFS[4], 16), 0)),
                  pl.BlockSpec((None, LANES, GLA_DK), lambda a, b: (l, 0, 0)),
                  pl.BlockSpec((1, GLA_DK), const2)],
        out_specs=[pl.BlockSpec((None, tm, D), lambda a, b: (a, b, 0)),
                   pl.BlockSpec((gb, rb, GLA_DK), lambda a, b: (a, b, 0))],
        out_shape=[jax.ShapeDtypeStruct((na, nr * tm, D), BF16),
                   jax.ShapeDtypeStruct((G, R, GLA_DK), F32)],
        compiler_params=_cparams("arbitrary", "arbitrary"),
        name="prenorm",
    )(x, mod, g, w_t, wal, bal)


def _seg_col(s):
    shift = jnp.where(s < SEG_GATE, 0,
                      jnp.where(s < SEG_MAIN, W_IN_OFFS[8] - SEG * SEG_GATE, W_IN_OFFS[5] - SEG * SEG_MAIN))
    return SEG * s + shift


def _inproj_kernel(u_ref, w_ref, qn_ref, kn_ref, proj_ref, qkv_ref):
    s = pl.program_id(2)
    tm = u_ref.shape[0]
    rc = min(tm, 512)

    @pl.when(s < SEG_MAIN)
    def _():
        proj_ref[...] = _dot_nt(u_ref[...], w_ref[...])

    def to_heads(r0, rows, val):
        for h in range(GROUP_HEADS):
            qkv_ref[h, r0:r0 + rows, :] = val[:, h * HEAD_DIM:(h + 1) * HEAD_DIM]

    def normed(gain_ref):
        g = gain_ref[...]
        for c in range(tm // rc):
            r = _dot_nt(u_ref[c * rc:(c + 1) * rc, :], w_ref[...])
            to_heads(c * rc, rc, jnp.concatenate(
                [_rms_lanes(r[:, h * HEAD_DIM:(h + 1) * HEAD_DIM], g) for h in range(GROUP_HEADS)],
                axis=1))

    @pl.when((s >= SEG_BQ) & (s < SEG_BK))
    def _():
        normed(qn_ref)

    @pl.when((s >= SEG_BK) & (s < SEG_BV))
    def _():
        normed(kn_ref)

    @pl.when(s >= SEG_BV)
    def _():
        to_heads(0, tm, _dot_nt(u_ref[...], w_ref[...]))


def _inproj_call(u, w_t, l, qn, kn, tm):
    A, N, D = u.shape
    const2 = lambda a, b, s: (0, 0)
    return pl.pallas_call(
        _inproj_kernel,
        grid=(A, N // tm, NSEG),
        in_specs=[pl.BlockSpec((None, tm, D), lambda a, b, s: (a, b, 0)),
                  pl.BlockSpec((pl.Element(SEG), pl.Element(D)),
                               lambda a, b, s: (pl.multiple_of(l * W_IN_COLS + _seg_col(s), 16), 0)),
                  pl.BlockSpec((1, HEAD_DIM), const2),
                  pl.BlockSpec((1, HEAD_DIM), const2)],
        out_specs=[pl.BlockSpec((None, tm, SEG), lambda a, b, s: (a, b, jnp.minimum(s, SEG_MAIN - 1))),
                   pl.BlockSpec((None, GROUP_HEADS, tm, HEAD_DIM),
                                lambda a, b, s: (a, jnp.clip(s - SEG_BQ, 0, NSEG - SEG_BQ - 1), b, 0))],
        out_shape=[jax.ShapeDtypeStruct((A, N, MAIN_W), F32),
                   jax.ShapeDtypeStruct((A, N_QKV, N, HEAD_DIM), F32)],
        compiler_params=_cparams("arbitrary", "arbitrary", "arbitrary"),
        name="in_proj",
    )(u, w_t, qn, kn)


def _gla_unit(q, k, v, la, S, tril, lmat, ones_c):
    dk = la.shape[1]
    la2 = jnp.concatenate(_split2(la), axis=1)
    b2 = _dot(lmat, la2)
    b = b2[:, :dk] + b2[:, dk:]
    half = b.shape[0] // 2
    b_mid = b[half - 1:half, :]
    qs = q * (GLA_DK_HEAD ** -0.5)
    qe = (qs * jnp.exp(b)).astype(BF16)
    qm = (qs * jnp.exp(b - b_mid)).astype(BF16)
    km = (k * jnp.exp(b_mid - b)).astype(BF16)
    vb = v.astype(BF16)
    sc = jnp.where(tril, _dot_nt(qm, km), 0.0)
    o = _dot(jnp.concatenate([qe, sc.astype(BF16)], axis=1),
             jnp.concatenate([S.astype(BF16), vb], axis=0))
    b_last = b[b.shape[0] - 1:, :]
    kd = (k * jnp.exp(b_last - b)).astype(BF16)
    d2 = _dot_tn(la2, ones_c)
    dec = jnp.exp(d2[:dk] + d2[dk:])
    s_new = S * jnp.concatenate([dec, dec], axis=1) + _dot_tn(kd, vb)
    return o, s_new


def _gla_prompt_kernel(q_ref, k_ref, v_ref, la_ref, gn_ref, o_ref, sfin_ref, s_scr):
    j = pl.program_id(0)
    nb, tb, _ = q_ref.shape
    C = GLA_CHUNK

    @pl.when(j == 0)
    def _():
        s_scr[...] = jnp.zeros_like(s_scr)

    row = lax.broadcasted_iota(jnp.int32, (C, C), 0)
    col = lax.broadcasted_iota(jnp.int32, (C, C), 1)
    tril = col <= row
    lmat = tril.astype(BF16)
    ones_c = jnp.ones((C, LANES), BF16)
    gn = gn_ref[...]
    for b in range(nb):
        for h in range(GLA_HEADS):
            ksl = slice(h * GLA_DK_HEAD, (h + 1) * GLA_DK_HEAD)
            vsl = slice(h * GLA_DV_HEAD, (h + 1) * GLA_DV_HEAD)
            S = s_scr[b, h]
            for c in range(tb // C):
                rows = slice(c * C, (c + 1) * C)
                o, S = _gla_unit(q_ref[b, rows, ksl], k_ref[b, rows, ksl], v_ref[b, rows, vsl],
                                 la_ref[b, rows, ksl], S, tril, lmat, ones_c)
                o_ref[b, rows, vsl] = _rms_lanes(o, gn)
            s_scr[b, h] = S

    @pl.when(j == pl.num_programs(0) - 1)
    def _():
        sfin_ref[...] = s_scr[...]


def _gla_prompt_call(proj, la, gn, tb):
    B, T, _ = proj.shape
    st = (B, GLA_HEADS, GLA_DK_HEAD, GLA_DV_HEAD)
    return pl.pallas_call(
        _gla_prompt_kernel,
        grid=(T // tb,),
        in_specs=[pl.BlockSpec((B, tb, GLA_DK), lambda j: (0, j, 0)),
                  pl.BlockSpec((B, tb, GLA_DK), lambda j: (0, j, 1)),
                  pl.BlockSpec((B, tb, GLA_DV), lambda j: (0, j, 1)),
                  pl.BlockSpec((B, tb, GLA_DK), lambda j: (0, j, 0)),
                  pl.BlockSpec((1, GLA_DV_HEAD), lambda j: (0, 0))],
        out_specs=[pl.BlockSpec((B, tb, GLA_DV), lambda j: (0, j, 0)),
                   pl.BlockSpec(st, lambda j: (0, 0, 0, 0))],
        out_shape=[jax.ShapeDtypeStruct((B, T, GLA_DV), F32), jax.ShapeDtypeStruct(st, F32)],
        scratch_shapes=[pltpu.VMEM(st, F32)],
        compiler_params=_cparams("arbitrary"),
        name="gla_prompt",
    )(proj, proj, proj, la, gn)


def _gla_sample_kernel(q_ref, k_ref, v_ref, la_ref, sin_ref, gn_ref, *rest):
    o_ref, sout_ref = rest[-2:]
    sb, ts, _ = q_ref.shape
    R = sb * ts
    row = lax.broadcasted_iota(jnp.int32, (R, R), 0)
    col = lax.broadcasted_iota(jnp.int32, (R, R), 1)
    shift = ts.bit_length() - 1
    same = jnp.right_shift(row, shift) == jnp.right_shift(col, shift)
    tril = same & (col <= row)
    lmat = tril.astype(BF16)
    bones = same.astype(BF16)
    ones_t = jnp.ones((ts, LANES), F32)
    gn = gn_ref[...]
    for h in range(GLA_HEADS):
        ksl = slice(h * GLA_DK_HEAD, (h + 1) * GLA_DK_HEAD)
        vsl = slice(h * GLA_DV_HEAD, (h + 1) * GLA_DV_HEAD)
        q = q_ref[:, :, ksl].reshape(R, GLA_DK_HEAD) * (GLA_DK_HEAD ** -0.5)
        k = k_ref[:, :, ksl].reshape(R, GLA_DK_HEAD)
        v = v_ref[:, :, vsl].reshape(R, GLA_DV_HEAD)
        la = la_ref[:, :, ksl].reshape(R, GLA_DK_HEAD)
        la_hi, la_lo = _split2(la)
        b = _dot(lmat, la_hi) + _dot(lmat, la_lo)
        btot = _dot(bones, la_hi) + _dot(bones, la_lo)
        qe = q * jnp.exp(b)
        ke = (k * jnp.exp(-b)).astype(BF16)
        kd = k * jnp.exp(btot - b)
        vb = v.astype(BF16)
        sc = jnp.where(tril, _dot_nt(qe.astype(BF16), ke), 0.0)
        o_intra = _dot(sc.astype(BF16), vb)
        la_hi32, la_lo32 = la_hi.astype(F32), la_lo.astype(F32)
        for i in range(sb):
            rs = slice(i * ts, (i + 1) * ts)
            S = sin_ref[i, h]
            o = _dot(qe[rs].astype(BF16), S.astype(BF16)) + o_intra[rs]
            kv = _dot_tn(kd[rs].astype(BF16), v[rs].astype(BF16))
            dec = jnp.exp(_dot_tn(la_hi32[rs].astype(BF16), ones_t.astype(BF16))
                          + _dot_tn(la_lo32[rs].astype(BF16), ones_t.astype(BF16)))
            sout_ref[i, h] = S * jnp.concatenate([dec, dec], axis=1) + kv
            o_ref[i, :, vsl] = _rms_lanes(o, gn)


def _gla_sample_call(proj, la, state, l, gn, sb, acc):
    NS, TS, _ = proj.shape
    st_shape = (None, sb, GLA_HEADS, GLA_DK_HEAD, GLA_DV_HEAD)
    st_spec = pl.BlockSpec(st_shape, lambda i: (l, i, 0, 0, 0))
    in_specs = [pl.BlockSpec((sb, TS, GLA_DK), lambda i: (i, 0, 0)),
                pl.BlockSpec((sb, TS, GLA_DK), lambda i: (i, 0, 1)),
                pl.BlockSpec((sb, TS, GLA_DV), lambda i: (i, 0, 1)),
                pl.BlockSpec((sb, TS, GLA_DK), lambda i: (i, 0, 0)),
                st_spec,
                pl.BlockSpec((1, GLA_DV_HEAD), lambda i: (0, 0))]
    args = [proj, proj, proj, la, state, gn]
    aliases = {}
    if acc is not None:
        in_specs.append(pl.BlockSpec(memory_space=pl.ANY))
        args.append(acc)
        aliases = {len(args) - 1: 1}
    return pl.pallas_call(
        _gla_sample_kernel,
        grid=(NS // sb,),
        in_specs=in_specs,
        out_specs=[pl.BlockSpec((sb, TS, GLA_DV), lambda i: (i, 0, 0)), st_spec],
        out_shape=[jax.ShapeDtypeStruct((NS, TS, GLA_DV), F32),
                   jax.ShapeDtypeStruct(state.shape, F32)],
        input_output_aliases=aliases,
        compiler_params=_cparams("arbitrary"),
        name="gla_sample",
    )(*args)


def _dil_prompt_kernel(q_ref, k_ref, v_ref, bias_ref, o_ref, lse_ref, kc_scr, vc_scr, *, d):
    n = pl.program_id(1)
    nsub = SUPER // (Q_BLOCK * d)
    scale = HEAD_DIM ** -0.5
    heads = range(GROUP_HEADS)

    @pl.when(n == 0)
    def _():
        kc_scr[...] = jnp.zeros_like(kc_scr)
        vc_scr[...] = jnp.zeros_like(vc_scr)

    col = lax.broadcasted_iota(jnp.int32, (Q_BLOCK, 2 * Q_BLOCK), 1)
    first_mask = jnp.where((col < Q_BLOCK) & (n == 0), NEG, 0.0)
    lane = lax.broadcasted_iota(jnp.int32, (Q_BLOCK, LANES), 1)
    ones = jnp.ones((Q_BLOCK, LANES), BF16)
    for p in range(d):
        cs = slice(p * Q_BLOCK, (p + 1) * Q_BLOCK)
        kp = [kc_scr[h, cs, :] for h in heads]
        vp = [vc_scr[h, cs, :] for h in heads]
        for i in range(nsub):
            start = p + d * Q_BLOCK * i
            rows = pl.ds(start, Q_BLOCK, stride=d) if d > 1 else pl.ds(start, Q_BLOCK)
            lses = []
            for h in heads:
                q = q_ref.at[h][rows, :].astype(BF16)
                kc = k_ref.at[h][rows, :].astype(BF16)
                vc = v_ref.at[h][rows, :].astype(BF16)
                bias = bias_ref[h] + first_mask if i == 0 else bias_ref[h]
                lg = _dot_nt(q, jnp.concatenate([kp[h], kc], axis=0)) * scale + bias
                m = jnp.max(lg, axis=-1, keepdims=True)
                pr = jnp.exp(lg - m).astype(BF16)
                v2 = jnp.concatenate([jnp.concatenate([vp[h], ones], axis=1),
                                      jnp.concatenate([vc, ones], axis=1)], axis=0)
                acc = _dot(pr, v2)
                s = acc[:, HEAD_DIM:]
                o_ref.at[h][rows, :] = acc[:, :HEAD_DIM] / s
                lses.append(m + jnp.log(s))
                kp[h], vp[h] = kc, vc
            packed = lses[-1]
            for h in reversed(heads[:-1]):
                packed = jnp.where(lane < (h + 1) * LSE_LANES, lses[h], packed)
            lse_ref[rows, :] = packed
        for h in heads:
            kc_scr[h, cs, :] = kp[h]
            vc_scr[h, cs, :] = vp[h]


def _dil_prompt_call(qkv, bias_g, g, d):
    B, _, T, _ = qkv.shape
    heads_blk = (None, GROUP_HEADS, SUPER, HEAD_DIM)

    def spec(seg):
        blk = seg - SEG_BQ + g
        return pl.BlockSpec(heads_blk, lambda b, n: (b, blk, n, 0))

    carry = pltpu.VMEM((GROUP_HEADS, d * Q_BLOCK, HEAD_DIM), BF16)
    return pl.pallas_call(
        functools.partial(_dil_prompt_kernel, d=d),
        grid=(B, T // SUPER),
        in_specs=[spec(SEG_BQ), spec(SEG_BK), spec(SEG_BV),
                  pl.BlockSpec((GROUP_HEADS, Q_BLOCK, 2 * Q_BLOCK), lambda b, n: (0, 0, 0))],
        out_specs=[pl.BlockSpec(heads_blk, lambda b, n: (b, 0, n, 0)),
                   pl.BlockSpec((None, SUPER, LANES), lambda b, n: (b, n, 0))],
        out_shape=[jax.ShapeDtypeStruct((B, GROUP_HEADS, T, HEAD_DIM), F32),
                   jax.ShapeDtypeStruct((B, T, LANES), F32)],
        scratch_shapes=[carry, carry],
        compiler_params=_cparams("arbitrary", "arbitrary"),
        name="dilated_prompt_g%d" % g,
    )(qkv, qkv, qkv, bias_g)


def _dil_sample_kernel(qkv_ref, c0_ref, c1_ref, c2_ref, bias_ref, o_ref):
    sb, ts, _ = o_ref.shape
    scale = HEAD_DIM ** -0.5
    qrows = GROUP_HEADS * ts
    n2 = c2_ref.shape[1] * c2_ref.shape[2]
    zpad = jnp.zeros((LANES - qrows, HEAD_DIM), F32)
    for j in range(sb):
        xs = (c0_ref[j].astype(BF16), c1_ref[j].astype(BF16),
              c2_ref[j].reshape(n2, LANES).astype(BF16))

        def planes(seg):
            return jnp.concatenate([qkv_ref[seg * GROUP_HEADS + h, j * ts:(j + 1) * ts, :]
                                    for h in range(GROUP_HEADS)], axis=0)

        qs = [planes(g).astype(BF16) for g in range(N_GROUPS)]
        parts = []
        off = 0
        for g in range(N_GROUPS):
            w = xs[g].shape[0]
            parts.append((_dot_nt(qs[g], xs[g]) * scale + bias_ref[:, off:off + w], xs[g], True))
            off += w
        for g in range(N_GROUPS):
            kn = jnp.concatenate([planes(N_GROUPS + g), zpad], axis=0).astype(BF16)
            vn = jnp.concatenate([planes(2 * N_GROUPS + g), zpad], axis=0).astype(BF16)
            parts.append((_dot_nt(qs[g], kn) * scale + bias_ref[:, off:off + LANES], vn, False))
            off += LANES
        m = parts[0][0].max(axis=-1, keepdims=True)
        for lg, _, _ in parts[1:]:
            m = jnp.maximum(m, lg.max(axis=-1, keepdims=True))
        s = jnp.zeros((qrows, 1), F32)
        acc = jnp.zeros((qrows, HEAD_DIM), F32)
        for lg, vv, cached in parts:
            p = jnp.exp(lg - m)
            s = s + jnp.sum(p, axis=-1, keepdims=True)
            if cached:
                p = pltpu.roll(p, GROUP_HEADS, axis=1)
            acc = acc + _dot(p.astype(BF16), vv)
        res = acc / s
        for h in range(GROUP_HEADS):
            o_ref[j, :, h * HEAD_DIM:(h + 1) * HEAD_DIM] = res[h * ts:(h + 1) * ts]


def _dil_sample_specs(caches, l, bias_s, ts, sb):
    c0, c1, c2 = caches
    in_specs = [pl.BlockSpec((None, N_QKV, sb * ts, HEAD_DIM), lambda i: (0, 0, i, 0)),
                pl.BlockSpec((None, sb) + c0.shape[2:], lambda i: (l, i, 0, 0)),
                pl.BlockSpec((None, sb) + c1.shape[2:], lambda i: (l, i, 0, 0)),
                pl.BlockSpec((None, sb, c2.shape[2], ts * CACHE_ROW, LANES),
                             lambda i: (l, i, 0, 0, 0)),
                pl.BlockSpec(bias_s.shape, lambda i: (0, 0))]
    return in_specs, pl.BlockSpec((sb, ts, B_OUT), lambda i: (i, 0, 0))


def _dil_sample_call(qkv, caches, l, bias_s, ns, ts, sb):
    in_specs, out_spec = _dil_sample_specs(caches, l, bias_s, ts, sb)
    return pl.pallas_call(
        _dil_sample_kernel,
        grid=(ns // sb,),
        in_specs=in_specs,
        out_specs=out_spec,
        out_shape=jax.ShapeDtypeStruct((ns, ts, B_OUT), F32),
        compiler_params=_cparams("arbitrary"),
        name="dilated_sample",
    )(qkv, *caches, bias_s)


def _merge_kernel(*refs, combine):
    if combine:
        (x_ref, mod_ref, oa_ref, gr_ref, ga_ref, gb_ref, o0_ref, o1_ref, o2_ref,
         l0_ref, l1_ref, l2_ref, wpa_ref, wpb_ref, wo_ref, out_ref) = refs
    else:
        (x_ref, mod_ref, oa_ref, gr_ref, ga_ref, gb_ref, ob_ref,
         wpa_ref, wpb_ref, wo_ref, out_ref) = refs
    gb, rb, d = x_ref.shape
    tm = gb * rb
    if combine:
        o_refs = (o0_ref, o1_ref, o2_ref)
        l_refs = (l0_ref, l1_ref, l2_ref)
        ls = [l[...].reshape(tm, LANES) for l in l_refs]
        m = jnp.maximum(jnp.maximum(ls[0], ls[1]), ls[2])
        es = [jnp.exp(l - m) for l in ls]
        den = es[0] + es[1] + es[2]
        ws = [e / den for e in es]
        parts = []
        for h in range(GROUP_HEADS):
            c = h * LSE_LANES
            parts.append(sum(ws[g][:, c:c + 1] * o_refs[g][h] for g in range(N_GROUPS)))
        ob = jnp.concatenate(parts, axis=1)
    else:
        ob = ob_ref[...].reshape(tm, B_OUT)
    gr = gr_ref[...].reshape(tm, GLA_DV)
    oa = oa_ref[...].reshape(tm, GLA_DV) * (gr * _sigmoid(gr))
    m1 = _dot(oa.astype(BF16), wpa_ref[...])
    m2 = _dot(ob.astype(BF16), wpb_ref[...])
    merged = (_sigmoid(ga_ref[...].reshape(tm, d)) * m1
              + _sigmoid(gb_ref[...].reshape(tm, d)) * m2)
    y = _dot(merged.astype(BF16), wo_ref[...]).reshape(gb, rb, d)
    out_ref[...] = x_ref[...] + mod_ref[...][:, 2:3, :] * y


def _merge_call(x, mod, oa, proj, ob_parts, wpa, wpb, wo, l, gb, rb):
    G, R, D = x.shape
    combine = len(ob_parts) > 1
    tm = gb * rb
    tok = lambda w, c: pl.BlockSpec((gb, rb, w), lambda a, b: (a, b, c))
    const = lambda shp: pl.BlockSpec((None,) + shp, lambda a, b: (l, 0, 0))
    in_specs = [tok(D, 0), pl.BlockSpec((gb, N_MOD, D), lambda a, b: (a, 0, 0)),
                tok(GLA_DV, 0), tok(GLA_DV, 2), tok(D, 3), tok(D, 4)]
    args = [x, mod, oa, proj, proj, proj]
    if combine:
        os_, ls_ = ob_parts
        head_major = pl.BlockSpec((None, GROUP_HEADS, tm, HEAD_DIM), lambda a, b: (a, 0, b, 0))
        in_specs += [head_major] * N_GROUPS + [tok(LANES, 0)] * N_GROUPS
        args += list(os_) + list(ls_)
    else:
        in_specs += [tok(B_OUT, 0)]
        args += list(ob_parts)
    in_specs += [const((GLA_DV, D)), const((B_OUT, D)), const((D, D))]
    args += [wpa, wpb, wo]
    return pl.pallas_call(
        functools.partial(_merge_kernel, combine=combine),
        grid=(G // gb, R // rb),
        in_specs=in_specs,
        out_specs=tok(D, 0),
        out_shape=jax.ShapeDtypeStruct((G, R, D), F32),
        compiler_params=_cparams("arbitrary", "arbitrary"),
        name="merge_combine" if combine else "merge",
    )(*args)


def _mlp_kernel(x_ref, mod_ref, g_ref, wu_ref, wd_ref, out_ref, u_scr, acc_scr):
    f = pl.program_id(2)
    gb, rb, d = x_ref.shape
    tm = gb * rb

    @pl.when(f == 0)
    def _():
        mod = mod_ref[...]
        u = _norm_mod(x_ref[...], g_ref[...], mod[:, 3:4, :], mod[:, 4:5, :])
        u_scr[...] = u.reshape(tm, d).astype(BF16)
        acc_scr[...] = jnp.zeros_like(acc_scr)

    hid = jnp.maximum(_dot(u_scr[...], wu_ref[...].astype(BF16)), 0.0)
    acc_scr[...] += _dot((hid * hid).astype(BF16), wd_ref[...].astype(BF16))

    @pl.when(f == pl.num_programs(2) - 1)
    def _():
        out_ref[...] = x_ref[...] + mod_ref[...][:, 5:6, :] * acc_scr[...].reshape(gb, rb, d)


def _mlp_call(x, mod, g, wu, wd, l, gb, rb, tf):
    G, R, D = x.shape
    F = wu.shape[2]
    tm = gb * rb
    return pl.pallas_call(
        _mlp_kernel,
        grid=(G // gb, R // rb, F // tf),
        in_specs=[pl.BlockSpec((gb, rb, D), lambda a, b, f: (a, b, 0)),
                  pl.BlockSpec((gb, N_MOD, D), lambda a, b, f: (a, 0, 0)),
                  pl.BlockSpec((1, D), lambda a, b, f: (0, 0)),
                  pl.BlockSpec((None, D, tf), lambda a, b, f: (l, 0, f)),
                  pl.BlockSpec((None, tf, D), lambda a, b, f: (l, f, 0))],
        out_specs=pl.BlockSpec((gb, rb, D), lambda a, b, f: (a, b, 0)),
        out_shape=jax.ShapeDtypeStruct((G, R, D), F32),
        scratch_shapes=[pltpu.VMEM((tm, D), BF16), pltpu.VMEM((tm, D), F32)],
        compiler_params=_cparams("arbitrary", "arbitrary", "arbitrary"),
        name="mlp",
    )(x, mod, g, wu, wd)


def _t5_bucket(dist):
    dist = np.asarray(dist)
    large = REL_MAX_EXACT + (np.log(np.maximum(dist, 1) / REL_MAX_EXACT)
                             / np.log(REL_MAX_DIST / REL_MAX_EXACT)
                             * (REL_BUCKETS - REL_MAX_EXACT)).astype(np.int32)
    large = np.minimum(large, REL_BUCKETS - 1)
    return np.where(dist < REL_MAX_EXACT, dist, large).astype(np.int32)


def _tap_bias(rel_bias, g):
    d = DIL_PAIRS[g][1]
    idx = _t5_bucket(d * np.arange(TAPS))
    return rel_bias[idx][:, g * GROUP_HEADS:(g + 1) * GROUP_HEADS].astype(F32)


def _prompt_bias(rel_bias, g):
    tb = _tap_bias(rel_bias, g)
    w = jnp.concatenate([tb[::-1].T, jnp.full((GROUP_HEADS, Q_BLOCK), NEG, F32)], axis=1)
    flat = jnp.tile(w, (1, Q_BLOCK))[:, :Q_BLOCK * 2 * Q_BLOCK]
    return flat.reshape(GROUP_HEADS, Q_BLOCK, 2 * Q_BLOCK)


def _dist_bias(tb, d, n):
    h = tb.shape[1]
    v = jnp.concatenate([tb[:, None, :], jnp.full((TAPS, d - 1, h), NEG, F32)], axis=1)
    v = v.reshape(TAPS * d, h)
    if n > TAPS * d:
        v = jnp.concatenate([v, jnp.full((n - TAPS * d, h), NEG, F32)], axis=0)
    return v[:n].T


def _sample_bias(rel_bias, ts, widths):
    H = GROUP_HEADS
    heads = np.arange(H)
    own = jnp.asarray(np.arange(CACHE_ROW)[None, :] == heads[:, None])
    same = jnp.asarray(heads[:, None] == heads[None, :])
    cache_parts, new_parts = [], []
    for g, (w, d) in enumerate(DIL_PAIRS):
        w_eff = widths[g]
        bv = _dist_bias(_tap_bias(rel_bias, g), d, w_eff + ts)
        rows = jnp.stack([bv[:, t + 1:t + 1 + w_eff][:, ::-1] for t in range(ts)], axis=1)
        if g == N_GROUPS - 1:
            rows = rows.reshape(H, ts, w_eff // d, d)[..., :ts].reshape(H, ts, (w_eff // d) * ts)
        tab = jnp.where(own[:, None, None, :], rows[..., None], NEG)
        cache_parts.append(tab.reshape(H * ts, -1))
        tri = jnp.stack([jnp.concatenate([bv[:, :t + 1][:, ::-1],
                                          jnp.full((H, ts - 1 - t), NEG, F32)], axis=1)
                         for t in range(ts)], axis=1)
        blk = jnp.where(same[:, None, :, None], tri[:, :, None, :], NEG).reshape(H * ts, H * ts)
        new_parts.append(jnp.concatenate([blk, jnp.full((H * ts, LANES - H * ts), NEG, F32)], axis=1))
    return jnp.concatenate(cache_parts + new_parts, axis=1)


def _window_rows(qkv, g, lo, lead):
    k0 = (SEG_BK - SEG_BQ + g) * GROUP_HEADS
    v0 = (SEG_BV - SEG_BQ + g) * GROUP_HEADS
    kv = jnp.stack([qkv[:, k0:k0 + GROUP_HEADS, lo:], qkv[:, v0:v0 + GROUP_HEADS, lo:]], axis=1)
    kv = jnp.transpose(kv, (0, 3, 1, 2, 4))
    return kv.reshape(lead + kv.shape[2:])


def kernel(x_prompt, x_sample, c_prompt, c_sample, state_gla, cache_win1, cache_win2, cache_win3,
           rel_bias, norm1_g, norm2_g, w_mod, b_mod, w_in, w_alpha, b_alpha, gla_norm_g,
           qn_g, kn_g, w_pa, w_pb, w_o, w_up, w_down):
    B, T, D = x_prompt.shape
    NS, TS, _ = x_sample.shape
    L = w_mod.shape[0]
    widths = [c.shape[2] for c in (cache_win1, cache_win2, cache_win3)]
    d2 = DIL_PAIRS[2][1]
    assert D == D_MODEL and TS == 8 and T % SUPER == 0
    assert widths[2] % d2 == 0 and TS <= d2
    caches = (cache_win1.astype(F32).reshape(L, NS, widths[0] * CACHE_ROW, LANES),
              cache_win2.astype(F32).reshape(L, NS, widths[1] * CACHE_ROW, LANES),
              cache_win3.astype(F32).reshape(L, NS, widths[2] // d2, d2 * CACHE_ROW, LANES))
    state = state_gla.astype(F32)

    mods = _mod_call(jnp.concatenate([c_prompt, c_sample], axis=0).astype(F32), w_mod, b_mod)
    mods = mods.reshape(L, B + NS, N_MOD, D)

    assert w_in.shape[2] == W_IN_COLS
    w_t = jnp.swapaxes(w_in, 1, 2).astype(BF16).reshape(L * W_IN_COLS, D)
    w_al = jnp.pad(w_alpha, ((0, 0), (0, LANES - GLA_RANK), (0, 0))).astype(BF16)
    w_pa_b, w_pb_b, w_o_b = w_pa.astype(BF16), w_pb.astype(BF16), w_o.astype(BF16)
    w_up32, w_down32 = w_up.astype(F32), w_down.astype(F32)

    bias_p = [_prompt_bias(rel_bias, g) for g in range(N_GROUPS)]
    bias_s = _sample_bias(rel_bias, TS, widths)

    tm = 1024
    xp = x_prompt.astype(F32)
    xs = x_sample.astype(F32)
    gla_p = []
    gla_s_all = None
    win_p = [[] for _ in range(N_GROUPS)]
    win_s = [[] for _ in range(N_GROUPS)]
    row2 = lambda v: v.reshape(1, -1)
    for l in range(L):
        mod_p, mod_s = mods[l, :B], mods[l, B:]
        pre = (row2(norm1_g[l]), w_t, w_al, l, row2(b_alpha[l]))
        qk_gain = (row2(qn_g[l]), row2(kn_g[l]))
        gn = row2(gla_norm_g[l])

        u, la = _prenorm_call(xp, mod_p, *pre, gb=1, rb=tm)
        proj, qkv = _inproj_call(u, w_t, l, *qk_gain, tm=2 * tm)
        oa, s_fin = _gla_prompt_call(proj, la, gn, tb=512)
        outs = [_dil_prompt_call(qkv, bias_p[g], g, DIL_PAIRS[g][1]) for g in range(N_GROUPS)]
        xp = _merge_call(xp, mod_p, oa, proj, ([o for o, _ in outs], [s for _, s in outs]),
                         w_pa_b, w_pb_b, w_o_b, l, gb=1, rb=tm // 2)
        xp = _mlp_call(xp, mod_p, row2(norm2_g[l]), w_up32, w_down32, l, gb=1, rb=tm, tf=1024)
        gla_p.append(s_fin)
        for g, (w, d) in enumerate(DIL_PAIRS):
            keep = min(w, T)
            win_p[g].append(_window_rows(qkv, g, T - keep, (B, keep)))

        u, la = _prenorm_call(xs, mod_s, *pre, gb=NS, rb=TS)
        proj, qkv = _inproj_call(u, w_t, l, *qk_gain, tm=NS * TS)
        proj = proj.reshape(NS, TS, MAIN_W)
        oa, gla_s_all = _gla_sample_call(proj, la, state, l, gn, sb=8, acc=gla_s_all)
        ob = _dil_sample_call(qkv, caches, l, bias_s, NS, TS, sb=2)
        xs = _merge_call(xs, mod_s, oa, proj, (ob,), w_pa_b, w_pb_b, w_o_b, l, gb=NS, rb=TS)
        xs = _mlp_call(xs, mod_s, row2(norm2_g[l]), w_up32, w_down32, l, gb=NS, rb=TS, tf=1024)
        for g in range(N_GROUPS):
            win_s[g].append(_window_rows(qkv, g, 0, (NS, TS)))

    return (xp.astype(x_prompt.dtype), xs.astype(x_sample.dtype), jnp.stack(gla_p), gla_s_all,
            jnp.stack(win_p[0]), jnp.stack(win_s[0]), jnp.stack(win_p[1]), jnp.stack(win_s[1]),
            jnp.stack(win_p[2]), jnp.stack(win_s[2]))
```

```python
import functools

import numpy as np
import jax
import jax.numpy as jnp
from jax import lax
from jax.experimental import pallas as pl
from jax.experimental.pallas import tpu as pltpu

F32 = jnp.float32
BF16 = jnp.bfloat16

D_MODEL = 1024
GLA_HEADS = 4
GLA_DK_HEAD = 128
GLA_DV_HEAD = 256
GLA_DK = GLA_HEADS * GLA_DK_HEAD
GLA_DV = GLA_HEADS * GLA_DV_HEAD
GLA_RANK = 16
GLA_GATE_NORM = 16.0
GLA_CHUNK = 128
DIL_PAIRS = ((128, 1), (512, 4), (2048, 16))
N_GROUPS = 3
GROUP_HEADS = 4
HEAD_DIM = 128
B_WIDTH = N_GROUPS * GROUP_HEADS * HEAD_DIM
B_OUT = GROUP_HEADS * HEAD_DIM
TAPS = 129
Q_BLOCK = 128
REL_BUCKETS = 32
REL_MAX_EXACT = 16
REL_MAX_DIST = 2048
D_FF = 4 * D_MODEL
N_MOD = 6
EPS = 1e-6
NEG = -1e30

LANES = 128
CACHE_ROW = 2 * GROUP_HEADS

W_IN_SIZES = (GLA_DK, GLA_DK, GLA_DV, GLA_DV, GLA_RANK, B_WIDTH, B_WIDTH, B_WIDTH, D_MODEL, D_MODEL)
W_IN_OFFS = tuple(int(v) for v in np.concatenate([[0], np.cumsum(W_IN_SIZES)]))
W_IN_COLS = W_IN_OFFS[-1]
SEG = 512
SEG_GATE = 6
SEG_MAIN = 10
NSEG = 19
MAIN_W = SEG_MAIN * SEG
SEG_BQ, SEG_BK, SEG_BV = 10, 13, 16
N_QKV = (NSEG - SEG_MAIN) * GROUP_HEADS
SUPER = 2048
LSE_LANES = LANES // GROUP_HEADS

VMEM_LIMIT = 48 * 1024 * 1024


def _cparams(*sem):
    return pltpu.CompilerParams(dimension_semantics=sem, vmem_limit_bytes=VMEM_LIMIT)


def _dot(a, b):
    return jnp.dot(a, b, preferred_element_type=F32)


def _dot_nt(a, b):
    return lax.dot_general(a, b, (((1,), (1,)), ((), ())), preferred_element_type=F32)


def _dot_tn(a, b):
    return lax.dot_general(a, b, (((0,), (0,)), ((), ())), preferred_element_type=F32)


def _sigmoid(x):
    return 1.0 / (1.0 + jnp.exp(-x))


def _norm_mod(x, g, shift, scale):
    ms = jnp.mean(x * x, axis=-1, keepdims=True)
    n = x * lax.rsqrt(ms + EPS) * g
    return n * (1.0 + scale) + shift


def _rms_lanes(x, g):
    ms = jnp.mean(x * x, axis=-1, keepdims=True)
    return x * lax.rsqrt(ms + EPS) * g


def _split2(x):
    hi = x.astype(BF16)
    lo = (x - hi.astype(F32)).astype(BF16)
    return hi, lo


def _mod_kernel(c_ref, w_ref, b_ref, o_ref):
    c = c_ref[...]
    a = (c * _sigmoid(c)).astype(BF16)
    o_ref[...] = _dot(a, w_ref[...].astype(BF16)) + b_ref[...]


def _mod_call(c_all, w_mod, b_mod):
    L, D, N = w_mod.shape
    nb = c_all.shape[0]
    tn = 1024
    return pl.pallas_call(
        _mod_kernel,
        grid=(L, N // tn),
        in_specs=[pl.BlockSpec((nb, D), lambda l, n: (0, 0)),
                  pl.BlockSpec((None, D, tn), lambda l, n: (l, 0, n)),
                  pl.BlockSpec((None, 1, tn), lambda l, n: (l, 0, n))],
        out_specs=pl.BlockSpec((None, nb, tn), lambda l, n: (l, 0, n)),
        out_shape=jax.ShapeDtypeStruct((L, nb, N), F32),
        compiler_params=_cparams("arbitrary", "arbitrary"),
        name="adaln_mod",
    )(c_all, w_mod, b_mod.reshape(L, 1, N))


def _prenorm_kernel(x_ref, mod_ref, g_ref, wglr_ref, wal_ref, bal_ref, u_ref, la_ref):
    gb, rb, d = x_ref.shape
    tm = gb * rb
    mod = mod_ref[...]
    u = _norm_mod(x_ref[...], g_ref[...], mod[:, 0:1, :], mod[:, 1:2, :])
    ub = u.reshape(tm, d).astype(BF16)
    u_ref[...] = ub
    glr = _dot_nt(ub, wglr_ref[...].astype(BF16))
    z = _dot(glr.astype(BF16), wal_ref[...]) + bal_ref[...]
    la = (jnp.minimum(z, 0.0) - jnp.log1p(jnp.exp(-jnp.abs(z)))) * (1.0 / GLA_GATE_NORM)
    la_ref[...] = la.reshape(gb, rb, GLA_DK)


def _prenorm_call(x, mod, g, w_t, wal, l, bal, gb, rb):
    G, R, D = x.shape
    tm = gb * rb
    na, nr = G // gb, R // rb
    const2 = lambda a, b: (0, 0)
    return pl.pallas_call(
        _prenorm_kernel,
        grid=(na, nr),
        in_specs=[pl.BlockSpec((gb, rb, D), lambda a, b: (a, b, 0)),
                  pl.BlockSpec((gb, N_MOD, D), lambda a, b: (a, 0, 0)),
                  pl.BlockSpec((1, D), const2),
                  pl.BlockSpec((pl.Element(LANES), pl.Element(D)),
                               lambda a, b: (pl.multiple_of(l * W_IN_COLS + W_IN_OFFS[4], 16), 0)),
                  pl.BlockSpec((None, LANES, GLA_DK), lambda a, b: (l, 0, 0)),
                  pl.BlockSpec((1, GLA_DK), const2)],
        out_specs=[pl.BlockSpec((None, tm, D), lambda a, b: (a, b, 0)),
                   pl.BlockSpec((gb, rb, GLA_DK), lambda a, b: (a, b, 0))],
        out_shape=[jax.ShapeDtypeStruct((na, nr * tm, D), BF16),
                   jax.ShapeDtypeStruct((G, R, GLA_DK), F32)],
        compiler_params=_cparams("arbitrary", "arbitrary"),
        name="prenorm",
    )(x, mod, g, w_t, wal, bal)


def _seg_col(s):
    shift = jnp.where(s < SEG_GATE, 0,
                      jnp.where(s < SEG_MAIN, W_IN_OFFS[8] - SEG * SEG_GATE, W_IN_OFFS[5] - SEG * SEG_MAIN))
    return SEG * s + shift


def _inproj_kernel(u_ref, w_ref, qn_ref, kn_ref, proj_ref, qkv_ref):
    s = pl.program_id(2)
    tm = u_ref.shape[0]
    rc = min(tm, 512)

    @pl.when(s < SEG_MAIN)
    def _():
        proj_ref[...] = _dot_nt(u_ref[...], w_ref[...].astype(BF16))

    def to_heads(r0, rows, val):
        for h in range(GROUP_HEADS):
            qkv_ref[h, r0:r0 + rows, :] = val[:, h * HEAD_DIM:(h + 1) * HEAD_DIM]

    def normed(gain_ref):
        g = gain_ref[...]
        w = w_ref[...].astype(BF16)
        for c in range(tm // rc):
            r = _dot_nt(u_ref[c * rc:(c + 1) * rc, :], w)
            to_heads(c * rc, rc, jnp.concatenate(
                [_rms_lanes(r[:, h * HEAD_DIM:(h + 1) * HEAD_DIM], g) for h in range(GROUP_HEADS)],
                axis=1))

    @pl.when((s >= SEG_BQ) & (s < SEG_BK))
    def _():
        normed(qn_ref)

    @pl.when((s >= SEG_BK) & (s < SEG_BV))
    def _():
        normed(kn_ref)

    @pl.when(s >= SEG_BV)
    def _():
        to_heads(0, tm, _dot_nt(u_ref[...], w_ref[...].astype(BF16)))


def _inproj_call(u, w_t, l, qn, kn, tm):
    A, N, D = u.shape
    const2 = lambda a, b, s: (0, 0)
    return pl.pallas_call(
        _inproj_kernel,
        grid=(A, N // tm, NSEG),
        in_specs=[pl.BlockSpec((None, tm, D), lambda a, b, s: (a, b, 0)),
                  pl.BlockSpec((pl.Element(SEG), pl.Element(D)),
                               lambda a, b, s: (pl.multiple_of(l * W_IN_COLS + _seg_col(s), 16), 0)),
                  pl.BlockSpec((1, HEAD_DIM), const2),
                  pl.BlockSpec((1, HEAD_DIM), const2)],
        out_specs=[pl.BlockSpec((None, tm, SEG), lambda a, b, s: (a, b, jnp.minimum(s, SEG_MAIN - 1))),
                   pl.BlockSpec((None, GROUP_HEADS, tm, HEAD_DIM),
                                lambda a, b, s: (a, jnp.clip(s - SEG_BQ, 0, NSEG - SEG_BQ - 1), b, 0))],
        out_shape=[jax.ShapeDtypeStruct((A, N, MAIN_W), F32),
                   jax.ShapeDtypeStruct((A, N_QKV, N, HEAD_DIM), F32)],
        compiler_params=_cparams("arbitrary", "arbitrary", "arbitrary"),
        name="in_proj",
    )(u, w_t, qn, kn)


def _gla_unit(q, k, v, la, S, tril, lmat, ones_c):
    dk = la.shape[1]
    la2 = jnp.concatenate(_split2(la), axis=1)
    b2 = _dot(lmat, la2)
    b = b2[:, :dk] + b2[:, dk:]
    half = b.shape[0] // 2
    b_mid = b[half - 1:half, :]
    qs = q * (GLA_DK_HEAD ** -0.5)
    qe = (qs * jnp.exp(b)).astype(BF16)
    qm = (qs * jnp.exp(b - b_mid)).astype(BF16)
    km = (k * jnp.exp(b_mid - b)).astype(BF16)
    vb = v.astype(BF16)
    sc = jnp.where(tril, _dot_nt(qm, km), 0.0)
    o = _dot(jnp.concatenate([qe, sc.astype(BF16)], axis=1),
             jnp.concatenate([S.astype(BF16), vb], axis=0))
    b_last = b[b.shape[0] - 1:, :]
    kd = (k * jnp.exp(b_last - b)).astype(BF16)
    d2 = _dot_tn(la2, ones_c)
    dec = jnp.exp(d2[:dk] + d2[dk:])
    s_new = S * jnp.concatenate([dec, dec], axis=1) + _dot_tn(kd, vb)
    return o, s_new


def _gla_prompt_kernel(q_ref, k_ref, v_ref, la_ref, gn_ref, o_ref, sfin_ref, s_scr):
    j = pl.program_id(0)
    nb, tb, _ = q_ref.shape
    C = GLA_CHUNK

    @pl.when(j == 0)
    def _():
        s_scr[...] = jnp.zeros_like(s_scr)

    row = lax.broadcasted_iota(jnp.int32, (C, C), 0)
    col = lax.broadcasted_iota(jnp.int32, (C, C), 1)
    tril = col <= row
    lmat = tril.astype(BF16)
    ones_c = jnp.ones((C, LANES), BF16)
    gn = gn_ref[...]
    for b in range(nb):
        for h in range(GLA_HEADS):
            ksl = slice(h * GLA_DK_HEAD, (h + 1) * GLA_DK_HEAD)
            vsl = slice(h * GLA_DV_HEAD, (h + 1) * GLA_DV_HEAD)
            S = s_scr[b, h]
            for c in range(tb // C):
                rows = slice(c * C, (c + 1) * C)
                o, S = _gla_unit(q_ref[b, rows, ksl], k_ref[b, rows, ksl], v_ref[b, rows, vsl],
                                 la_ref[b, rows, ksl], S, tril, lmat, ones_c)
                o_ref[b, rows, vsl] = _rms_lanes(o, gn)
            s_scr[b, h] = S

    @pl.when(j == pl.num_programs(0) - 1)
    def _():
        sfin_ref[...] = s_scr[...]


def _gla_prompt_call(proj, la, gn, tb):
    B, T, _ = proj.shape
    st = (B, GLA_HEADS, GLA_DK_HEAD, GLA_DV_HEAD)
    return pl.pallas_call(
        _gla_prompt_kernel,
        grid=(T // tb,),
        in_specs=[pl.BlockSpec((B, tb, GLA_DK), lambda j: (0, j, 0)),
                  pl.BlockSpec((B, tb, GLA_DK), lambda j: (0, j, 1)),
                  pl.BlockSpec((B, tb, GLA_DV), lambda j: (0, j, 1)),
                  pl.BlockSpec((B, tb, GLA_DK), lambda j: (0, j, 0)),
                  pl.BlockSpec((1, GLA_DV_HEAD), lambda j: (0, 0))],
        out_specs=[pl.BlockSpec((B, tb, GLA_DV), lambda j: (0, j, 0)),
                   pl.BlockSpec(st, lambda j: (0, 0, 0, 0))],
        out_shape=[jax.ShapeDtypeStruct((B, T, GLA_DV), F32), jax.ShapeDtypeStruct(st, F32)],
        scratch_shapes=[pltpu.VMEM(st, F32)],
        compiler_params=_cparams("arbitrary"),
        name="gla_prompt",
    )(proj, proj, proj, la, gn)


def _gla_sample_kernel(q_ref, k_ref, v_ref, la_ref, sin_ref, gn_ref, *rest):
    o_ref, sout_ref = rest[-2:]
    sb, ts, _ = q_ref.shape
    R = sb * ts
    row = lax.broadcasted_iota(jnp.int32, (R, R), 0)
    col = lax.broadcasted_iota(jnp.int32, (R, R), 1)
    shift = ts.bit_length() - 1
    same = jnp.right_shift(row, shift) == jnp.right_shift(col, shift)
    tril = same & (col <= row)
    lmat = tril.astype(BF16)
    bones = same.astype(BF16)
    ones_t = jnp.ones((ts, LANES), F32)
    gn = gn_ref[...]
    for h in range(GLA_HEADS):
        ksl = slice(h * GLA_DK_HEAD, (h + 1) * GLA_DK_HEAD)
        vsl = slice(h * GLA_DV_HEAD, (h + 1) * GLA_DV_HEAD)
        q = q_ref[:, :, ksl].reshape(R, GLA_DK_HEAD) * (GLA_DK_HEAD ** -0.5)
        k = k_ref[:, :, ksl].reshape(R, GLA_DK_HEAD)
        v = v_ref[:, :, vsl].reshape(R, GLA_DV_HEAD)
        la = la_ref[:, :, ksl].reshape(R, GLA_DK_HEAD)
        la_hi, la_lo = _split2(la)
        b = _dot(lmat, la_hi) + _dot(lmat, la_lo)
        btot = _dot(bones, la_hi) + _dot(bones, la_lo)
        qe = q * jnp.exp(b)
        ke = (k * jnp.exp(-b)).astype(BF16)
        kd = k * jnp.exp(btot - b)
        vb = v.astype(BF16)
        sc = jnp.where(tril, _dot_nt(qe.astype(BF16), ke), 0.0)
        o_intra = _dot(sc.astype(BF16), vb)
        la_hi32, la_lo32 = la_hi.astype(F32), la_lo.astype(F32)
        for i in range(sb):
            rs = slice(i * ts, (i + 1) * ts)
            S = sin_ref[i, h]
            o = _dot(qe[rs].astype(BF16), S.astype(BF16)) + o_intra[rs]
            kv = _dot_tn(kd[rs].astype(BF16), v[rs].astype(BF16))
            dec = jnp.exp(_dot_tn(la_hi32[rs].astype(BF16), ones_t.astype(BF16))
                          + _dot_tn(la_lo32[rs].astype(BF16), ones_t.astype(BF16)))
            sout_ref[i, h] = S * jnp.concatenate([dec, dec], axis=1) + kv
            o_ref[i, :, vsl] = _rms_lanes(o, gn)


def _gla_sample_call(proj, la, state, l, gn, sb, acc):
    NS, TS, _ = proj.shape
    st_shape = (None, sb, GLA_HEADS, GLA_DK_HEAD, GLA_DV_HEAD)
    st_spec = pl.BlockSpec(st_shape, lambda i: (l, i, 0, 0, 0))
    in_specs = [pl.BlockSpec((sb, TS, GLA_DK), lambda i: (i, 0, 0)),
                pl.BlockSpec((sb, TS, GLA_DK), lambda i: (i, 0, 1)),
                pl.BlockSpec((sb, TS, GLA_DV), lambda i: (i, 0, 1)),
                pl.BlockSpec((sb, TS, GLA_DK), lambda i: (i, 0, 0)),
                st_spec,
                pl.BlockSpec((1, GLA_DV_HEAD), lambda i: (0, 0))]
    args = [proj, proj, proj, la, state, gn]
    aliases = {}
    if acc is not None:
        in_specs.append(pl.BlockSpec(memory_space=pl.ANY))
        args.append(acc)
        aliases = {len(args) - 1: 1}
    return pl.pallas_call(
        _gla_sample_kernel,
        grid=(NS // sb,),
        in_specs=in_specs,
        out_specs=[pl.BlockSpec((sb, TS, GLA_DV), lambda i: (i, 0, 0)), st_spec],
        out_shape=[jax.ShapeDtypeStruct((NS, TS, GLA_DV), F32),
                   jax.ShapeDtypeStruct(state.shape, F32)],
        input_output_aliases=aliases,
        compiler_params=_cparams("arbitrary"),
        name="gla_sample",
    )(*args)


def _dil_prompt_kernel(q_ref, k_ref, v_ref, bias_ref, o_ref, lse_ref, kc_scr, vc_scr, *, d):
    n = pl.program_id(1)
    nsub = SUPER // (Q_BLOCK * d)
    scale = HEAD_DIM ** -0.5
    heads = range(GROUP_HEADS)

    @pl.when(n == 0)
    def _():
        kc_scr[...] = jnp.zeros_like(kc_scr)
        vc_scr[...] = jnp.zeros_like(vc_scr)

    col = lax.broadcasted_iota(jnp.int32, (Q_BLOCK, 2 * Q_BLOCK), 1)
    first_mask = jnp.where((col < Q_BLOCK) & (n == 0), NEG, 0.0)
    lane = lax.broadcasted_iota(jnp.int32, (Q_BLOCK, LANES), 1)
    ones = jnp.ones((Q_BLOCK, LANES), BF16)
    for p in range(d):
        cs = slice(p * Q_BLOCK, (p + 1) * Q_BLOCK)
        kp = [kc_scr[h, cs, :] for h in heads]
        vp = [vc_scr[h, cs, :] for h in heads]
        for i in range(nsub):
            start = p + d * Q_BLOCK * i
            rows = pl.ds(start, Q_BLOCK, stride=d) if d > 1 else pl.ds(start, Q_BLOCK)
            lses = []
            for h in heads:
                q = q_ref.at[h][rows, :].astype(BF16)
                kc = k_ref.at[h][rows, :].astype(BF16)
                vc = v_ref.at[h][rows, :].astype(BF16)
                bias = bias_ref[h] + first_mask if i == 0 else bias_ref[h]
                lg = _dot_nt(q, jnp.concatenate([kp[h], kc], axis=0)) * scale + bias
                m = jnp.max(lg, axis=-1, keepdims=True)
                pr = jnp.exp(lg - m).astype(BF16)
                v2 = jnp.concatenate([jnp.concatenate([vp[h], ones], axis=1),
                                      jnp.concatenate([vc, ones], axis=1)], axis=0)
                acc = _dot(pr, v2)
                s = acc[:, HEAD_DIM:]
                o_ref.at[h][rows, :] = acc[:, :HEAD_DIM] / s
                lses.append(m + jnp.log(s))
                kp[h], vp[h] = kc, vc
            packed = lses[-1]
            for h in reversed(heads[:-1]):
                packed = jnp.where(lane < (h + 1) * LSE_LANES, lses[h], packed)
            lse_ref[rows, :] = packed
        for h in heads:
            kc_scr[h, cs, :] = kp[h]
            vc_scr[h, cs, :] = vp[h]


def _dil_prompt_call(qkv, bias_g, g, d):
    B, _, T, _ = qkv.shape
    heads_blk = (None, GROUP_HEADS, SUPER, HEAD_DIM)

    def spec(seg):
        blk = seg - SEG_BQ + g
        return pl.BlockSpec(heads_blk, lambda b, n: (b, blk, n, 0))

    carry = pltpu.VMEM((GROUP_HEADS, d * Q_BLOCK, HEAD_DIM), BF16)
    return pl.pallas_call(
        functools.partial(_dil_prompt_kernel, d=d),
        grid=(B, T // SUPER),
        in_specs=[spec(SEG_BQ), spec(SEG_BK), spec(SEG_BV),
                  pl.BlockSpec((GROUP_HEADS, Q_BLOCK, 2 * Q_BLOCK), lambda b, n: (0, 0, 0))],
        out_specs=[pl.BlockSpec(heads_blk, lambda b, n: (b, 0, n, 0)),
                   pl.BlockSpec((None, SUPER, LANES), lambda b, n: (b, n, 0))],
        out_shape=[jax.ShapeDtypeStruct((B, GROUP_HEADS, T, HEAD_DIM), F32),
                   jax.ShapeDtypeStruct((B, T, LANES), F32)],
        scratch_shapes=[carry, carry],
        compiler_params=_cparams("arbitrary", "arbitrary"),
        name="dilated_prompt_g%d" % g,
    )(qkv, qkv, qkv, bias_g)


def _dil_sample_kernel(qkv_ref, c0_ref, c1_ref, c2_ref, bias_ref, o_ref):
    sb, ts, _ = o_ref.shape
    scale = HEAD_DIM ** -0.5
    heads = range(GROUP_HEADS)
    qrows = GROUP_HEADS * ts
    widths = (c0_ref.shape[1] // CACHE_ROW, c1_ref.shape[1] // CACHE_ROW, c2_ref.shape[1] * ts)
    zpad = jnp.zeros((LANES - qrows, HEAD_DIM), F32)
    for j in range(sb):
        def cache_rows(g, r):
            if g == 0:
                return c0_ref.at[j][pl.ds(r, widths[0], stride=CACHE_ROW), :].astype(BF16)
            if g == 1:
                return c1_ref.at[j][pl.ds(r, widths[1], stride=CACHE_ROW), :].astype(BF16)
            rows = c2_ref.at[j][:, pl.ds(r, ts, stride=CACHE_ROW), :]
            return rows.reshape(widths[2], HEAD_DIM).astype(BF16)

        def plane(seg, h):
            return qkv_ref[seg * GROUP_HEADS + h, j * ts:(j + 1) * ts, :]

        def planes(seg):
            return jnp.concatenate([plane(seg, h) for h in heads], axis=0)

        parts = []
        off = 0
        for g in range(N_GROUPS):
            lg = jnp.concatenate([_dot_nt(plane(g, h).astype(BF16), cache_rows(g, h)) for h in heads],
                                 axis=0)
            parts.append((lg * scale + bias_ref[:, off:off + widths[g]],
                          [cache_rows(g, GROUP_HEADS + h) for h in heads]))
            off += widths[g]
        for g in range(N_GROUPS):
            kn = jnp.concatenate([planes(N_GROUPS + g), zpad], axis=0).astype(BF16)
            vn = jnp.concatenate([planes(2 * N_GROUPS + g), zpad], axis=0).astype(BF16)
            parts.append((_dot_nt(planes(g).astype(BF16), kn) * scale + bias_ref[:, off:off + LANES], vn))
            off += LANES
        m = parts[0][0].max(axis=-1, keepdims=True)
        for lg, _ in parts[1:]:
            m = jnp.maximum(m, lg.max(axis=-1, keepdims=True))
        s = jnp.zeros((qrows, 1), F32)
        acc = jnp.zeros((qrows, HEAD_DIM), F32)
        for lg, vv in parts:
            p = jnp.exp(lg - m)
            s = s + jnp.sum(p, axis=-1, keepdims=True)
            if isinstance(vv, list):
                acc = acc + jnp.concatenate(
                    [_dot(p[h * ts:(h + 1) * ts].astype(BF16), vv[h]) for h in heads], axis=0)
            else:
                acc = acc + _dot(p.astype(BF16), vv)
        res = acc / s
        for h in range(GROUP_HEADS):
            o_ref[j, :, h * HEAD_DIM:(h + 1) * HEAD_DIM] = res[h * ts:(h + 1) * ts]


def _dil_sample_specs(caches, l, bias_s, ts, sb):
    c0, c1, c2 = caches
    in_specs = [pl.BlockSpec((None, N_QKV, sb * ts, HEAD_DIM), lambda i: (0, 0, i, 0)),
                pl.BlockSpec((None, sb) + c0.shape[2:], lambda i: (l, i, 0, 0)),
                pl.BlockSpec((None, sb) + c1.shape[2:], lambda i: (l, i, 0, 0)),
                pl.BlockSpec((None, sb, c2.shape[2], ts * CACHE_ROW, LANES),
                             lambda i: (l, i, 0, 0, 0)),
                pl.BlockSpec(bias_s.shape, lambda i: (0, 0))]
    return in_specs, pl.BlockSpec((sb, ts, B_OUT), lambda i: (i, 0, 0))


def _dil_sample_call(qkv, caches, l, bias_s, ns, ts, sb):
    in_specs, out_spec = _dil_sample_specs(caches, l, bias_s, ts, sb)
    return pl.pallas_call(
        _dil_sample_kernel,
        grid=(ns // sb,),
        in_specs=in_specs,
        out_specs=out_spec,
        out_shape=jax.ShapeDtypeStruct((ns, ts, B_OUT), F32),
        compiler_params=_cparams("arbitrary"),
        name="dilated_sample",
    )(qkv, *caches, bias_s)


def _merge_kernel(*refs, combine):
    if combine:
        (x_ref, mod_ref, oa_ref, gr_ref, ga_ref, gb_ref, o0_ref, o1_ref, o2_ref,
         l0_ref, l1_ref, l2_ref, wpa_ref, wpb_ref, wo_ref, out_ref) = refs
    else:
        (x_ref, mod_ref, oa_ref, gr_ref, ga_ref, gb_ref, ob_ref,
         wpa_ref, wpb_ref, wo_ref, out_ref) = refs
    gb, rb, d = x_ref.shape
    tm = gb * rb
    if combine:
        o_refs = (o0_ref, o1_ref, o2_ref)
        l_refs = (l0_ref, l1_ref, l2_ref)
        ls = [l[...].reshape(tm, LANES) for l in l_refs]
        m = jnp.maximum(jnp.maximum(ls[0], ls[1]), ls[2])
        es = [jnp.exp(l - m) for l in ls]
        den = es[0] + es[1] + es[2]
        ws = [e / den for e in es]
        parts = []
        for h in range(GROUP_HEADS):
            c = h * LSE_LANES
            parts.append(sum(ws[g][:, c:c + 1] * o_refs[g][h] for g in range(N_GROUPS)))
        ob = jnp.concatenate(parts, axis=1)
    else:
        ob = ob_ref[...].reshape(tm, B_OUT)
    gr = gr_ref[...].reshape(tm, GLA_DV)
    oa = oa_ref[...].reshape(tm, GLA_DV) * (gr * _sigmoid(gr))
    m1 = _dot(oa.astype(BF16), wpa_ref[...])
    m2 = _dot(ob.astype(BF16), wpb_ref[...])
    merged = (_sigmoid(ga_ref[...].reshape(tm, d)) * m1
              + _sigmoid(gb_ref[...].reshape(tm, d)) * m2)
    y = _dot(merged.astype(BF16), wo_ref[...]).reshape(gb, rb, d)
    out_ref[...] = x_ref[...] + mod_ref[...][:, 2:3, :] * y


def _merge_call(x, mod, oa, proj, ob_parts, wpa, wpb, wo, l, gb, rb):
    G, R, D = x.shape
    combine = len(ob_parts) > 1
    tm = gb * rb
    tok = lambda w, c: pl.BlockSpec((gb, rb, w), lambda a, b: (a, b, c))
    const = lambda shp: pl.BlockSpec((None,) + shp, lambda a, b: (l, 0, 0))
    in_specs = [tok(D, 0), pl.BlockSpec((gb, N_MOD, D), lambda a, b: (a, 0, 0)),
                tok(GLA_DV, 0), tok(GLA_DV, 2), tok(D, 3), tok(D, 4)]
    args = [x, mod, oa, proj, proj, proj]
    if combine:
        os_, ls_ = ob_parts
        head_major = pl.BlockSpec((None, GROUP_HEADS, tm, HEAD_DIM), lambda a, b: (a, 0, b, 0))
        in_specs += [head_major] * N_GROUPS + [tok(LANES, 0)] * N_GROUPS
        args += list(os_) + list(ls_)
    else:
        in_specs += [tok(B_OUT, 0)]
        args += list(ob_parts)
    in_specs += [const((GLA_DV, D)), const((B_OUT, D)), const((D, D))]
    args += [wpa, wpb, wo]
    return pl.pallas_call(
        functools.partial(_merge_kernel, combine=combine),
        grid=(G // gb, R // rb),
        in_specs=in_specs,
        out_specs=tok(D, 0),
        out_shape=jax.ShapeDtypeStruct((G, R, D), F32),
        compiler_params=_cparams("arbitrary", "arbitrary"),
        name="merge_combine" if combine else "merge",
    )(*args)


def _mlp_kernel(x_ref, mod_ref, g_ref, wu_ref, wd_ref, out_ref, u_scr, acc_scr):
    f = pl.program_id(2)
    gb, rb, d = x_ref.shape
    tm = gb * rb

    @pl.when(f == 0)
    def _():
        mod = mod_ref[...]
        u = _norm_mod(x_ref[...], g_ref[...], mod[:, 3:4, :], mod[:, 4:5, :])
        u_scr[...] = u.reshape(tm, d).astype(BF16)
        acc_scr[...] = jnp.zeros_like(acc_scr)

    hid = jnp.maximum(_dot(u_scr[...], wu_ref[...].astype(BF16)), 0.0)
    acc_scr[...] += _dot((hid * hid).astype(BF16), wd_ref[...].astype(BF16))

    @pl.when(f == pl.num_programs(2) - 1)
    def _():
        out_ref[...] = x_ref[...] + mod_ref[...][:, 5:6, :] * acc_scr[...].reshape(gb, rb, d)


def _mlp_call(x, mod, g, wu, wd, l, gb, rb, tf):
    G, R, D = x.shape
    F = wu.shape[2]
    tm = gb * rb
    return pl.pallas_call(
        _mlp_kernel,
        grid=(G // gb, R // rb, F // tf),
        in_specs=[pl.BlockSpec((gb, rb, D), lambda a, b, f: (a, b, 0)),
                  pl.BlockSpec((gb, N_MOD, D), lambda a, b, f: (a, 0, 0)),
                  pl.BlockSpec((1, D), lambda a, b, f: (0, 0)),
                  pl.BlockSpec((None, D, tf), lambda a, b, f: (l, 0, f)),
                  pl.BlockSpec((None, tf, D), lambda a, b, f: (l, f, 0))],
        out_specs=pl.BlockSpec((gb, rb, D), lambda a, b, f: (a, b, 0)),
        out_shape=jax.ShapeDtypeStruct((G, R, D), F32),
        scratch_shapes=[pltpu.VMEM((tm, D), BF16), pltpu.VMEM((tm, D), F32)],
        compiler_params=_cparams("arbitrary", "arbitrary", "arbitrary"),
        name="mlp",
    )(x, mod, g, wu, wd)


def _t5_bucket(dist):
    dist = np.asarray(dist)
    large = REL_MAX_EXACT + (np.log(np.maximum(dist, 1) / REL_MAX_EXACT)
                             / np.log(REL_MAX_DIST / REL_MAX_EXACT)
                             * (REL_BUCKETS - REL_MAX_EXACT)).astype(np.int32)
    large = np.minimum(large, REL_BUCKETS - 1)
    return np.where(dist < REL_MAX_EXACT, dist, large).astype(np.int32)


def _tap_bias(rel_bias, g):
    d = DIL_PAIRS[g][1]
    idx = _t5_bucket(d * np.arange(TAPS))
    return rel_bias[idx][:, g * GROUP_HEADS:(g + 1) * GROUP_HEADS].astype(F32)


def _prompt_bias(rel_bias, g):
    tb = _tap_bias(rel_bias, g)
    w = jnp.concatenate([tb[::-1].T, jnp.full((GROUP_HEADS, Q_BLOCK), NEG, F32)], axis=1)
    flat = jnp.tile(w, (1, Q_BLOCK))[:, :Q_BLOCK * 2 * Q_BLOCK]
    return flat.reshape(GROUP_HEADS, Q_BLOCK, 2 * Q_BLOCK)


def _dist_bias(tb, d, n):
    h = tb.shape[1]
    v = jnp.concatenate([tb[:, None, :], jnp.full((TAPS, d - 1, h), NEG, F32)], axis=1)
    v = v.reshape(TAPS * d, h)
    if n > TAPS * d:
        v = jnp.concatenate([v, jnp.full((n - TAPS * d, h), NEG, F32)], axis=0)
    return v[:n].T


def _sample_bias(rel_bias, ts, widths):
    H = GROUP_HEADS
    heads = np.arange(H)
    same = jnp.asarray(heads[:, None] == heads[None, :])
    cache_parts, new_parts = [], []
    for g, (w, d) in enumerate(DIL_PAIRS):
        w_eff = widths[g]
        bv = _dist_bias(_tap_bias(rel_bias, g), d, w_eff + ts)
        rows = jnp.stack([bv[:, t + 1:t + 1 + w_eff][:, ::-1] for t in range(ts)], axis=1)
        if g == N_GROUPS - 1:
            rows = rows.reshape(H, ts, w_eff // d, d)[..., :ts].reshape(H, ts, (w_eff // d) * ts)
        cache_parts.append(rows.reshape(H * ts, -1))
        tri = jnp.stack([jnp.concatenate([bv[:, :t + 1][:, ::-1],
                                          jnp.full((H, ts - 1 - t), NEG, F32)], axis=1)
                         for t in range(ts)], axis=1)
        blk = jnp.where(same[:, None, :, None], tri[:, :, None, :], NEG).reshape(H * ts, H * ts)
        new_parts.append(jnp.concatenate([blk, jnp.full((H * ts, LANES - H * ts), NEG, F32)], axis=1))
    return jnp.concatenate(cache_parts + new_parts, axis=1)


def _window_rows(qkv, g, lo, lead):
    k0 = (SEG_BK - SEG_BQ + g) * GROUP_HEADS
    v0 = (SEG_BV - SEG_BQ + g) * GROUP_HEADS
    kv = jnp.stack([qkv[:, k0:k0 + GROUP_HEADS, lo:], qkv[:, v0:v0 + GROUP_HEADS, lo:]], axis=1)
    kv = jnp.transpose(kv, (0, 3, 1, 2, 4))
    return kv.reshape(lead + kv.shape[2:])


def kernel(x_prompt, x_sample, c_prompt, c_sample, state_gla, cache_win1, cache_win2, cache_win3,
           rel_bias, norm1_g, norm2_g, w_mod, b_mod, w_in, w_alpha, b_alpha, gla_norm_g,
           qn_g, kn_g, w_pa, w_pb, w_o, w_up, w_down):
    B, T, D = x_prompt.shape
    NS, TS, _ = x_sample.shape
    L = w_mod.shape[0]
    widths = [c.shape[2] for c in (cache_win1, cache_win2, cache_win3)]
    d2 = DIL_PAIRS[2][1]
    assert D == D_MODEL and TS == 8 and T % SUPER == 0
    assert widths[2] % d2 == 0 and TS <= d2
    caches = (cache_win1.astype(F32).reshape(L, NS, widths[0] * CACHE_ROW, LANES),
              cache_win2.astype(F32).reshape(L, NS, widths[1] * CACHE_ROW, LANES),
              cache_win3.astype(F32).reshape(L, NS, widths[2] // d2, d2 * CACHE_ROW, LANES))
    state = state_gla.astype(F32)

    mods = _mod_call(jnp.concatenate([c_prompt, c_sample], axis=0).astype(F32), w_mod, b_mod)
    mods = mods.reshape(L, B + NS, N_MOD, D)

    assert w_in.shape[2] == W_IN_COLS
    w_t = jnp.swapaxes(w_in, 1, 2).astype(F32).reshape(L * W_IN_COLS, D)
    w_al = jnp.pad(w_alpha, ((0, 0), (0, LANES - GLA_RANK), (0, 0))).astype(BF16)
    w_pa_b, w_pb_b, w_o_b = w_pa.astype(BF16), w_pb.astype(BF16), w_o.astype(BF16)
    w_up32, w_down32 = w_up.astype(F32), w_down.astype(F32)

    bias_p = [_prompt_bias(rel_bias, g) for g in range(N_GROUPS)]
    bias_s = _sample_bias(rel_bias, TS, widths)

    tm = 1024
    xp = x_prompt.astype(F32)
    xs = x_sample.astype(F32)
    gla_p = []
    gla_s_all = None
    win_p = [[] for _ in range(N_GROUPS)]
    win_s = [[] for _ in range(N_GROUPS)]
    row2 = lambda v: v.reshape(1, -1)
    for l in range(L):
        mod_p, mod_s = mods[l, :B], mods[l, B:]
        pre = (row2(norm1_g[l]), w_t, w_al, l, row2(b_alpha[l]))
        qk_gain = (row2(qn_g[l]), row2(kn_g[l]))
        gn = row2(gla_norm_g[l])

        u, la = _prenorm_call(xp, mod_p, *pre, gb=1, rb=tm)
        proj, qkv = _inproj_call(u, w_t, l, *qk_gain, tm=2 * tm)
        oa, s_fin = _gla_prompt_call(proj, la, gn, tb=512)
        outs = [_dil_prompt_call(qkv, bias_p[g], g, DIL_PAIRS[g][1]) for g in range(N_GROUPS)]
        xp = _merge_call(xp, mod_p, oa, proj, ([o for o, _ in outs], [s for _, s in outs]),
                         w_pa_b, w_pb_b, w_o_b, l, gb=1, rb=tm // 2)
        xp = _mlp_call(xp, mod_p, row2(norm2_g[l]), w_up32, w_down32, l, gb=1, rb=tm, tf=1024)
        gla_p.append(s_fin)
        for g, (w, d) in enumerate(DIL_PAIRS):
            keep = min(w, T)
            win_p[g].append(_window_rows(qkv, g, T - keep, (B, keep)))

        u, la = _prenorm_call(xs, mod_s, *pre, gb=NS, rb=TS)
        proj, qkv = _inproj_call(u, w_t, l, *qk_gain, tm=NS * TS)
        proj = proj.reshape(NS, TS, MAIN_W)
        oa, gla_s_all = _gla_sample_call(proj, la, state, l, gn, sb=8, acc=gla_s_all)
        ob = _dil_sample_call(qkv, caches, l, bias_s, NS, TS, sb=2)
        xs = _merge_call(xs, mod_s, oa, proj, (ob,), w_pa_b, w_pb_b, w_o_b, l, gb=NS, rb=TS)
        xs = _mlp_call(xs, mod_s, row2(norm2_g[l]), w_up32, w_down32, l, gb=NS, rb=TS, tf=1024)
        for g in range(N_GROUPS):
            win_s[g].append(_window_rows(qkv, g, 0, (NS, TS)))

    return (xp.astype(x_prompt.dtype), xs.astype(x_sample.dtype), jnp.stack(gla_p), gla_s_all,
            jnp.stack(win_p[0]), jnp.stack(win_s[0]), jnp.stack(win_p[1]), jnp.stack(win_s[1]),
            jnp.stack(win_p[2]), jnp.stack(win_s[2]))
```

```python
import functools

import numpy as np
import jax
import jax.numpy as jnp
from jax import lax
from jax.experimental import pallas as pl
from jax.experimental.pallas import tpu as pltpu

F32 = jnp.float32
BF16 = jnp.bfloat16

D_MODEL = 1024
GLA_HEADS = 4
GLA_DK_HEAD = 128
GLA_DV_HEAD = 256
GLA_DK = GLA_HEADS * GLA_DK_HEAD
GLA_DV = GLA_HEADS * GLA_DV_HEAD
GLA_RANK = 16
GLA_GATE_NORM = 16.0
GLA_CHUNK = 128
DIL_PAIRS = ((128, 1), (512, 4), (2048, 16))
N_GROUPS = 3
GROUP_HEADS = 4
HEAD_DIM = 128
B_WIDTH = N_GROUPS * GROUP_HEADS * HEAD_DIM
B_OUT = GROUP_HEADS * HEAD_DIM
TAPS = 129
Q_BLOCK = 128
REL_BUCKETS = 32
REL_MAX_EXACT = 16
REL_MAX_DIST = 2048
D_FF = 4 * D_MODEL
N_MOD = 6
EPS = 1e-6
NEG = -1e30

LANES = 128
CACHE_ROW = 2 * GROUP_HEADS

W_IN_SIZES = (GLA_DK, GLA_DK, GLA_DV, GLA_DV, GLA_RANK, B_WIDTH, B_WIDTH, B_WIDTH, D_MODEL, D_MODEL)
W_IN_OFFS = tuple(int(v) for v in np.concatenate([[0], np.cumsum(W_IN_SIZES)]))
W_IN_COLS = W_IN_OFFS[-1]
MAIN_SEG = 1024
MAIN_W = 2 * GLA_DK + 2 * GLA_DV + 2 * D_MODEL
SEG_BQ, SEG_BK, SEG_BV = 0, N_GROUPS, 2 * N_GROUPS
N_QKV = 3 * N_GROUPS * GROUP_HEADS
SUPER = 2048
LSE_LANES = LANES // GROUP_HEADS

VMEM_LIMIT = 48 * 1024 * 1024


def _cparams(*sem):
    return pltpu.CompilerParams(dimension_semantics=sem, vmem_limit_bytes=VMEM_LIMIT)


def _dot(a, b):
    return jnp.dot(a, b, preferred_element_type=F32)


def _dot_nt(a, b):
    return lax.dot_general(a, b, (((1,), (1,)), ((), ())), preferred_element_type=F32)


def _dot_tn(a, b):
    return lax.dot_general(a, b, (((0,), (0,)), ((), ())), preferred_element_type=F32)


def _sigmoid(x):
    return 1.0 / (1.0 + jnp.exp(-x))


def _norm_mod(x, g, shift, scale):
    ms = jnp.mean(x * x, axis=-1, keepdims=True)
    n = x * lax.rsqrt(ms + EPS) * g
    return n * (1.0 + scale) + shift


def _rms_lanes(x, g):
    ms = jnp.mean(x * x, axis=-1, keepdims=True)
    return x * lax.rsqrt(ms + EPS) * g


def _split2(x):
    hi = x.astype(BF16)
    lo = (x - hi.astype(F32)).astype(BF16)
    return hi, lo


def _mod_kernel(c_ref, w_ref, b_ref, o_ref):
    c = c_ref[...]
    a = (c * _sigmoid(c)).astype(BF16)
    o_ref[...] = _dot(a, w_ref[...].astype(BF16)) + b_ref[...]


def _mod_call(c_all, w_mod, b_mod):
    L, D, N = w_mod.shape
    nb = c_all.shape[0]
    tn = 1024
    return pl.pallas_call(
        _mod_kernel,
        grid=(L, N // tn),
        in_specs=[pl.BlockSpec((nb, D), lambda l, n: (0, 0)),
                  pl.BlockSpec((None, D, tn), lambda l, n: (l, 0, n)),
                  pl.BlockSpec((None, 1, tn), lambda l, n: (l, 0, n))],
        out_specs=pl.BlockSpec((None, nb, tn), lambda l, n: (l, 0, n)),
        out_shape=jax.ShapeDtypeStruct((L, nb, N), F32),
        compiler_params=_cparams("arbitrary", "arbitrary"),
        name="adaln_mod",
    )(c_all, w_mod, b_mod.reshape(L, 1, N))


def _prenorm_kernel(x_ref, mod_ref, g_ref, wglr_ref, wal_ref, bal_ref, u_ref, la_ref):
    gb, rb, d = x_ref.shape
    tm = gb * rb
    mod = mod_ref[...]
    u = _norm_mod(x_ref[...], g_ref[...], mod[:, 0:1, :], mod[:, 1:2, :])
    ub = u.reshape(tm, d).astype(BF16)
    u_ref[...] = ub
    glr = _dot_nt(ub, wglr_ref[...].astype(BF16))
    z = _dot(glr.astype(BF16), wal_ref[...]) + bal_ref[...]
    la = (jnp.minimum(z, 0.0) - jnp.log1p(jnp.exp(-jnp.abs(z)))) * (1.0 / GLA_GATE_NORM)
    la_ref[...] = la.reshape(gb, rb, GLA_DK)


def _prenorm_call(x, mod, g, w_t, wal, l, bal, gb, rb):
    G, R, D = x.shape
    tm = gb * rb
    na, nr = G // gb, R // rb
    const2 = lambda a, b: (0, 0)
    return pl.pallas_call(
        _prenorm_kernel,
        grid=(na, nr),
        in_specs=[pl.BlockSpec((gb, rb, D), lambda a, b: (a, b, 0)),
                  pl.BlockSpec((gb, N_MOD, D), lambda a, b: (a, 0, 0)),
                  pl.BlockSpec((1, D), const2),
                  pl.BlockSpec((pl.Element(LANES), pl.Element(D)),
                               lambda a, b: (pl.multiple_of(l * W_IN_COLS + W_IN_OFFS[4], 16), 0)),
                  pl.BlockSpec((None, LANES, GLA_DK), lambda a, b: (l, 0, 0)),
                  pl.BlockSpec((1, GLA_DK), const2)],
        out_specs=[pl.BlockSpec((None, tm, D), lambda a, b: (a, b, 0)),
                   pl.BlockSpec((gb, rb, GLA_DK), lambda a, b: (a, b, 0))],
        out_shape=[jax.ShapeDtypeStruct((na, nr * tm, D), BF16),
                   jax.ShapeDtypeStruct((G, R, GLA_DK), F32)],
        compiler_params=_cparams("arbitrary", "arbitrary"),
        name="prenorm",
    )(x, mod, g, w_t, wal, bal)


def _inproj_main_kernel(u_ref, w_ref, proj_ref):
    proj_ref[...] = _dot_nt(u_ref[...], w_ref[...].astype(BF16))


def _inproj_main_call(u, w_t, l, tm):
    A, N, D = u.shape
    n_gla = W_IN_OFFS[4] // MAIN_SEG

    def col(s):
        return MAIN_SEG * s + jnp.where(s < n_gla, 0, W_IN_OFFS[8] - W_IN_OFFS[4])

    return pl.pallas_call(
        _inproj_main_kernel,
        grid=(A, N // tm, MAIN_W // MAIN_SEG),
        in_specs=[pl.BlockSpec((None, tm, D), lambda a, b, s: (a, b, 0)),
                  pl.BlockSpec((pl.Element(MAIN_SEG), pl.Element(D)),
                               lambda a, b, s: (pl.multiple_of(l * W_IN_COLS + col(s), 16), 0))],
        out_specs=pl.BlockSpec((None, tm, MAIN_SEG), lambda a, b, s: (a, b, s)),
        out_shape=jax.ShapeDtypeStruct((A, N, MAIN_W), F32),
        compiler_params=_cparams("arbitrary", "arbitrary", "arbitrary"),
        name="in_proj_main",
    )(u, w_t)


def _inproj_qkv_kernel(u_ref, w_ref, qn_ref, kn_ref, qkv_ref):
    s = pl.program_id(2)
    tm = u_ref.shape[0]
    rc = min(tm, 512)
    nh = B_WIDTH // HEAD_DIM

    def chunks(finish):
        w = w_ref[...].astype(BF16)
        for c in range(tm // rc):
            r = _dot_nt(u_ref[c * rc:(c + 1) * rc, :], w)
            for h in range(nh):
                qkv_ref[h, c * rc:(c + 1) * rc, :] = finish(r[:, h * HEAD_DIM:(h + 1) * HEAD_DIM])

    @pl.when(s == 0)
    def _():
        chunks(lambda r: _rms_lanes(r, qn_ref[...]))

    @pl.when(s == 1)
    def _():
        chunks(lambda r: _rms_lanes(r, kn_ref[...]))

    @pl.when(s == 2)
    def _():
        chunks(lambda r: r)


def _inproj_qkv_call(u, w_t, l, qn, kn, tm):
    A, N, D = u.shape
    nh = B_WIDTH // HEAD_DIM
    const2 = lambda a, b, s: (0, 0)
    return pl.pallas_call(
        _inproj_qkv_kernel,
        grid=(A, N // tm, N_QKV // nh),
        in_specs=[pl.BlockSpec((None, tm, D), lambda a, b, s: (a, b, 0)),
                  pl.BlockSpec((pl.Element(B_WIDTH), pl.Element(D)),
                               lambda a, b, s: (pl.multiple_of(
                                   l * W_IN_COLS + W_IN_OFFS[5] + B_WIDTH * s, 16), 0)),
                  pl.BlockSpec((1, HEAD_DIM), const2),
                  pl.BlockSpec((1, HEAD_DIM), const2)],
        out_specs=pl.BlockSpec((None, nh, tm, HEAD_DIM), lambda a, b, s: (a, s, b, 0)),
        out_shape=jax.ShapeDtypeStruct((A, N_QKV, N, HEAD_DIM), F32),
        compiler_params=_cparams("arbitrary", "arbitrary", "arbitrary"),
        name="in_proj_qkv",
    )(u, w_t, qn, kn)


def _gla_unit(q, k, v, la, S, tril, lmat, ones_c):
    dk = la.shape[1]
    la2 = jnp.concatenate(_split2(la), axis=1)
    b2 = _dot(lmat, la2)
    b = b2[:, :dk] + b2[:, dk:]
    half = b.shape[0] // 2
    b_mid = b[half - 1:half, :]
    qs = q * (GLA_DK_HEAD ** -0.5)
    qe = (qs * jnp.exp(b)).astype(BF16)
    qm = (qs * jnp.exp(b - b_mid)).astype(BF16)
    km = (k * jnp.exp(b_mid - b)).astype(BF16)
    vb = v.astype(BF16)
    sc = jnp.where(tril, _dot_nt(qm, km), 0.0)
    o = _dot(jnp.concatenate([qe, sc.astype(BF16)], axis=1),
             jnp.concatenate([S.astype(BF16), vb], axis=0))
    b_last = b[b.shape[0] - 1:, :]
    kd = (k * jnp.exp(b_last - b)).astype(BF16)
    d2 = _dot_tn(la2, ones_c)
    dec = jnp.exp(d2[:dk] + d2[dk:])
    s_new = S * jnp.concatenate([dec, dec], axis=1) + _dot_tn(kd, vb)
    return o, s_new


def _gla_prompt_kernel(q_ref, k_ref, v_ref, la_ref, gn_ref, o_ref, sfin_ref, s_scr):
    j = pl.program_id(0)
    nb, tb, _ = q_ref.shape
    C = GLA_CHUNK

    @pl.when(j == 0)
    def _():
        s_scr[...] = jnp.zeros_like(s_scr)

    row = lax.broadcasted_iota(jnp.int32, (C, C), 0)
    col = lax.broadcasted_iota(jnp.int32, (C, C), 1)
    tril = col <= row
    lmat = tril.astype(BF16)
    ones_c = jnp.ones((C, LANES), BF16)
    gn = gn_ref[...]
    for b in range(nb):
        for h in range(GLA_HEADS):
            ksl = slice(h * GLA_DK_HEAD, (h + 1) * GLA_DK_HEAD)
            vsl = slice(h * GLA_DV_HEAD, (h + 1) * GLA_DV_HEAD)
            S = s_scr[b, h]
            for c in range(tb // C):
                rows = slice(c * C, (c + 1) * C)
                o, S = _gla_unit(q_ref[b, rows, ksl], k_ref[b, rows, ksl], v_ref[b, rows, vsl],
                                 la_ref[b, rows, ksl], S, tril, lmat, ones_c)
                o_ref[b, rows, vsl] = _rms_lanes(o, gn)
            s_scr[b, h] = S

    @pl.when(j == pl.num_programs(0) - 1)
    def _():
        sfin_ref[...] = s_scr[...]


def _gla_prompt_call(proj, la, gn, tb):
    B, T, _ = proj.shape
    st = (B, GLA_HEADS, GLA_DK_HEAD, GLA_DV_HEAD)
    return pl.pallas_call(
        _gla_prompt_kernel,
        grid=(T // tb,),
        in_specs=[pl.BlockSpec((B, tb, GLA_DK), lambda j: (0, j, 0)),
                  pl.BlockSpec((B, tb, GLA_DK), lambda j: (0, j, 1)),
                  pl.BlockSpec((B, tb, GLA_DV), lambda j: (0, j, 1)),
                  pl.BlockSpec((B, tb, GLA_DK), lambda j: (0, j, 0)),
                  pl.BlockSpec((1, GLA_DV_HEAD), lambda j: (0, 0))],
        out_specs=[pl.BlockSpec((B, tb, GLA_DV), lambda j: (0, j, 0)),
                   pl.BlockSpec(st, lambda j: (0, 0, 0, 0))],
        out_shape=[jax.ShapeDtypeStruct((B, T, GLA_DV), F32), jax.ShapeDtypeStruct(st, F32)],
        scratch_shapes=[pltpu.VMEM(st, F32)],
        compiler_params=_cparams("arbitrary"),
        name="gla_prompt",
    )(proj, proj, proj, la, gn)


def _gla_sample_kernel(q_ref, k_ref, v_ref, la_ref, sin_ref, gn_ref, *rest):
    o_ref, sout_ref = rest[-2:]
    sb, ts, _ = q_ref.shape
    R = sb * ts
    row = lax.broadcasted_iota(jnp.int32, (R, R), 0)
    col = lax.broadcasted_iota(jnp.int32, (R, R), 1)
    shift = ts.bit_length() - 1
    same = jnp.right_shift(row, shift) == jnp.right_shift(col, shift)
    tril = same & (col <= row)
    lmat = tril.astype(BF16)
    bones = same.astype(BF16)
    ones_t = jnp.ones((ts, LANES), F32)
    gn = gn_ref[...]
    for h in range(GLA_HEADS):
        ksl = slice(h * GLA_DK_HEAD, (h + 1) * GLA_DK_HEAD)
        vsl = slice(h * GLA_DV_HEAD, (h + 1) * GLA_DV_HEAD)
        q = q_ref[:, :, ksl].reshape(R, GLA_DK_HEAD) * (GLA_DK_HEAD ** -0.5)
        k = k_ref[:, :, ksl].reshape(R, GLA_DK_HEAD)
        v = v_ref[:, :, vsl].reshape(R, GLA_DV_HEAD)
        la = la_ref[:, :, ksl].reshape(R, GLA_DK_HEAD)
        la_hi, la_lo = _split2(la)
        b = _dot(lmat, la_hi) + _dot(lmat, la_lo)
        btot = _dot(bones, la_hi) + _dot(bones, la_lo)
        qe = q * jnp.exp(b)
        ke = (k * jnp.exp(-b)).astype(BF16)
        kd = k * jnp.exp(btot - b)
        vb = v.astype(BF16)
        sc = jnp.where(tril, _dot_nt(qe.astype(BF16), ke), 0.0)
        o_intra = _dot(sc.astype(BF16), vb)
        la_hi32, la_lo32 = la_hi.astype(F32), la_lo.astype(F32)
        for i in range(sb):
            rs = slice(i * ts, (i + 1) * ts)
            S = sin_ref[i, h]
            o = _dot(qe[rs].astype(BF16), S.astype(BF16)) + o_intra[rs]
            kv = _dot_tn(kd[rs].astype(BF16), v[rs].astype(BF16))
            dec = jnp.exp(_dot_tn(la_hi32[rs].astype(BF16), ones_t.astype(BF16))
                          + _dot_tn(la_lo32[rs].astype(BF16), ones_t.astype(BF16)))
            sout_ref[i, h] = S * jnp.concatenate([dec, dec], axis=1) + kv
            o_ref[i, :, vsl] = _rms_lanes(o, gn)


def _gla_sample_call(proj, la, state, l, gn, sb, acc):
    NS, TS, _ = proj.shape
    st_shape = (None, sb, GLA_HEADS, GLA_DK_HEAD, GLA_DV_HEAD)
    st_spec = pl.BlockSpec(st_shape, lambda i: (l, i, 0, 0, 0))
    in_specs = [pl.BlockSpec((sb, TS, GLA_DK), lambda i: (i, 0, 0)),
                pl.BlockSpec((sb, TS, GLA_DK), lambda i: (i, 0, 1)),
                pl.BlockSpec((sb, TS, GLA_DV), lambda i: (i, 0, 1)),
                pl.BlockSpec((sb, TS, GLA_DK), lambda i: (i, 0, 0)),
                st_spec,
                pl.BlockSpec((1, GLA_DV_HEAD), lambda i: (0, 0))]
    args = [proj, proj, proj, la, state, gn]
    aliases = {}
    if acc is not None:
        in_specs.append(pl.BlockSpec(memory_space=pl.ANY))
        args.append(acc)
        aliases = {len(args) - 1: 1}
    return pl.pallas_call(
        _gla_sample_kernel,
        grid=(NS // sb,),
        in_specs=in_specs,
        out_specs=[pl.BlockSpec((sb, TS, GLA_DV), lambda i: (i, 0, 0)), st_spec],
        out_shape=[jax.ShapeDtypeStruct((NS, TS, GLA_DV), F32),
                   jax.ShapeDtypeStruct(state.shape, F32)],
        input_output_aliases=aliases,
        compiler_params=_cparams("arbitrary"),
        name="gla_sample",
    )(*args)


def _dil_prompt_kernel(q_ref, k_ref, v_ref, bias_ref, o_ref, lse_ref, kc_scr, vc_scr, *, d):
    n = pl.program_id(1)
    nsub = SUPER // (Q_BLOCK * d)
    scale = HEAD_DIM ** -0.5
    heads = range(GROUP_HEADS)

    @pl.when(n == 0)
    def _():
        kc_scr[...] = jnp.zeros_like(kc_scr)
        vc_scr[...] = jnp.zeros_like(vc_scr)

    col = lax.broadcasted_iota(jnp.int32, (Q_BLOCK, 2 * Q_BLOCK), 1)
    first_mask = jnp.where((col < Q_BLOCK) & (n == 0), NEG, 0.0)
    lane = lax.broadcasted_iota(jnp.int32, (Q_BLOCK, LANES), 1)
    ones = jnp.ones((Q_BLOCK, LANES), BF16)
    for p in range(d):
        cs = slice(p * Q_BLOCK, (p + 1) * Q_BLOCK)
        kp = [kc_scr[h, cs, :] for h in heads]
        vp = [vc_scr[h, cs, :] for h in heads]
        for i in range(nsub):
            start = p + d * Q_BLOCK * i
            rows = pl.ds(start, Q_BLOCK, stride=d) if d > 1 else pl.ds(start, Q_BLOCK)
            lses = []
            for h in heads:
                q = q_ref.at[h][rows, :].astype(BF16)
                kc = k_ref.at[h][rows, :].astype(BF16)
                vc = v_ref.at[h][rows, :].astype(BF16)
                bias = bias_ref[h] + first_mask if i == 0 else bias_ref[h]
                lg = _dot_nt(q, jnp.concatenate([kp[h], kc], axis=0)) * scale + bias
                m = jnp.max(lg, axis=-1, keepdims=True)
                pr = jnp.exp(lg - m).astype(BF16)
                v2 = jnp.concatenate([jnp.concatenate([vp[h], ones], axis=1),
                                      jnp.concatenate([vc, ones], axis=1)], axis=0)
                acc = _dot(pr, v2)
                s = acc[:, HEAD_DIM:]
                o_ref.at[h][rows, :] = acc[:, :HEAD_DIM] / s
                lses.append(m + jnp.log(s))
                kp[h], vp[h] = kc, vc
            packed = lses[-1]
            for h in reversed(heads[:-1]):
                packed = jnp.where(lane < (h + 1) * LSE_LANES, lses[h], packed)
            lse_ref[rows, :] = packed
        for h in heads:
            kc_scr[h, cs, :] = kp[h]
            vc_scr[h, cs, :] = vp[h]


def _dil_prompt_call(qkv, bias_g, g, d):
    B, _, T, _ = qkv.shape
    heads_blk = (None, GROUP_HEADS, SUPER, HEAD_DIM)

    def spec(seg):
        blk = seg - SEG_BQ + g
        return pl.BlockSpec(heads_blk, lambda b, n: (b, blk, n, 0))

    carry = pltpu.VMEM((GROUP_HEADS, d * Q_BLOCK, HEAD_DIM), BF16)
    return pl.pallas_call(
        functools.partial(_dil_prompt_kernel, d=d),
        grid=(B, T // SUPER),
        in_specs=[spec(SEG_BQ), spec(SEG_BK), spec(SEG_BV),
                  pl.BlockSpec((GROUP_HEADS, Q_BLOCK, 2 * Q_BLOCK), lambda b, n: (0, 0, 0))],
        out_specs=[pl.BlockSpec(heads_blk, lambda b, n: (b, 0, n, 0)),
                   pl.BlockSpec((None, SUPER, LANES), lambda b, n: (b, n, 0))],
        out_shape=[jax.ShapeDtypeStruct((B, GROUP_HEADS, T, HEAD_DIM), F32),
                   jax.ShapeDtypeStruct((B, T, LANES), F32)],
        scratch_shapes=[carry, carry],
        compiler_params=_cparams("arbitrary", "arbitrary"),
        name="dilated_prompt_g%d" % g,
    )(qkv, qkv, qkv, bias_g)


def _dil_sample_kernel(qkv_ref, c0_ref, c1_ref, c2_ref, bias_ref, o_ref):
    sb, ts, _ = o_ref.shape
    scale = HEAD_DIM ** -0.5
    heads = range(GROUP_HEADS)
    qrows = GROUP_HEADS * ts
    widths = (c0_ref.shape[1] // CACHE_ROW, c1_ref.shape[1] // CACHE_ROW, c2_ref.shape[1] * ts)
    zpad = jnp.zeros((LANES - qrows, HEAD_DIM), F32)
    for j in range(sb):
        def cache_rows(g, r):
            if g == 0:
                return c0_ref.at[j][pl.ds(r, widths[0], stride=CACHE_ROW), :].astype(BF16)
            if g == 1:
                return c1_ref.at[j][pl.ds(r, widths[1], stride=CACHE_ROW), :].astype(BF16)
            rows = c2_ref.at[j][:, pl.ds(r, ts, stride=CACHE_ROW), :]
            return rows.reshape(widths[2], HEAD_DIM).astype(BF16)

        def plane(seg, h):
            return qkv_ref[seg * GROUP_HEADS + h, j * ts:(j + 1) * ts, :]

        def planes(seg):
            return jnp.concatenate([plane(seg, h) for h in heads], axis=0)

        parts = []
        off = 0
        for g in range(N_GROUPS):
            lg = jnp.concatenate([_dot_nt(plane(g, h).astype(BF16), cache_rows(g, h)) for h in heads],
                                 axis=0)
            parts.append((lg * scale + bias_ref[:, off:off + widths[g]],
                          [cache_rows(g, GROUP_HEADS + h) for h in heads]))
            off += widths[g]
        for g in range(N_GROUPS):
            kn = jnp.concatenate([planes(N_GROUPS + g), zpad], axis=0).astype(BF16)
            vn = jnp.concatenate([planes(2 * N_GROUPS + g), zpad], axis=0).astype(BF16)
            parts.append((_dot_nt(planes(g).astype(BF16), kn) * scale + bias_ref[:, off:off + LANES], vn))
            off += LANES
        m = parts[0][0].max(axis=-1, keepdims=True)
        for lg, _ in parts[1:]:
            m = jnp.maximum(m, lg.max(axis=-1, keepdims=True))
        s = jnp.zeros((qrows, 1), F32)
        acc = jnp.zeros((qrows, HEAD_DIM), F32)
        for lg, vv in parts:
            p = jnp.exp(lg - m)
            s = s + jnp.sum(p, axis=-1, keepdims=True)
            if isinstance(vv, list):
                acc = acc + jnp.concatenate(
                    [_dot(p[h * ts:(h + 1) * ts].astype(BF16), vv[h]) for h in heads], axis=0)
            else:
                acc = acc + _dot(p.astype(BF16), vv)
        res = acc / s
        for h in range(GROUP_HEADS):
            o_ref[j, :, h * HEAD_DIM:(h + 1) * HEAD_DIM] = res[h * ts:(h + 1) * ts]


def _dil_sample_specs(caches, l, bias_s, ts, sb):
    c0, c1, c2 = caches
    in_specs = [pl.BlockSpec((None, N_QKV, sb * ts, HEAD_DIM), lambda i: (0, 0, i, 0)),
                pl.BlockSpec((None, sb) + c0.shape[2:], lambda i: (l, i, 0, 0)),
                pl.BlockSpec((None, sb) + c1.shape[2:], lambda i: (l, i, 0, 0)),
                pl.BlockSpec((None, sb, c2.shape[2], ts * CACHE_ROW, LANES),
                             lambda i: (l, i, 0, 0, 0)),
                pl.BlockSpec(bias_s.shape, lambda i: (0, 0))]
    return in_specs, pl.BlockSpec((sb, ts, B_OUT), lambda i: (i, 0, 0))


def _dil_sample_call(qkv, caches, l, bias_s, ns, ts, sb):
    in_specs, out_spec = _dil_sample_specs(caches, l, bias_s, ts, sb)
    return pl.pallas_call(
        _dil_sample_kernel,
        grid=(ns // sb,),
        in_specs=in_specs,
        out_specs=out_spec,
        out_shape=jax.ShapeDtypeStruct((ns, ts, B_OUT), F32),
        compiler_params=_cparams("arbitrary"),
        name="dilated_sample",
    )(qkv, *caches, bias_s)


def _merge_kernel(*refs, combine):
    if combine:
        (x_ref, mod_ref, oa_ref, gr_ref, ga_ref, gb_ref, o0_ref, o1_ref, o2_ref,
         l0_ref, l1_ref, l2_ref, wpa_ref, wpb_ref, wo_ref, out_ref) = refs
    else:
        (x_ref, mod_ref, oa_ref, gr_ref, ga_ref, gb_ref, ob_ref,
         wpa_ref, wpb_ref, wo_ref, out_ref) = refs
    gb, rb, d = x_ref.shape
    tm = gb * rb
    if combine:
        o_refs = (o0_ref, o1_ref, o2_ref)
        l_refs = (l0_ref, l1_ref, l2_ref)
        ls = [l[...].reshape(tm, LANES) for l in l_refs]
        m = jnp.maximum(jnp.maximum(ls[0], ls[1]), ls[2])
        es = [jnp.exp(l - m) for l in ls]
        den = es[0] + es[1] + es[2]
        ws = [e / den for e in es]
        parts = []
        for h in range(GROUP_HEADS):
            c = h * LSE_LANES
            parts.append(sum(ws[g][:, c:c + 1] * o_refs[g][h] for g in range(N_GROUPS)))
        ob = jnp.concatenate(parts, axis=1)
    else:
        ob = ob_ref[...].reshape(tm, B_OUT)
    gr = gr_ref[...].reshape(tm, GLA_DV)
    oa = oa_ref[...].reshape(tm, GLA_DV) * (gr * _sigmoid(gr))
    m1 = _dot(oa.astype(BF16), wpa_ref[...])
    m2 = _dot(ob.astype(BF16), wpb_ref[...])
    merged = (_sigmoid(ga_ref[...].reshape(tm, d)) * m1
              + _sigmoid(gb_ref[...].reshape(tm, d)) * m2)
    y = _dot(merged.astype(BF16), wo_ref[...]).reshape(gb, rb, d)
    out_ref[...] = x_ref[...] + mod_ref[...][:, 2:3, :] * y


def _merge_call(x, mod, oa, proj, ob_parts, wpa, wpb, wo, l, gb, rb):
    G, R, D = x.shape
    combine = len(ob_parts) > 1
    tm = gb * rb
    tok = lambda w, c: pl.BlockSpec((gb, rb, w), lambda a, b: (a, b, c))
    const = lambda shp: pl.BlockSpec((None,) + shp, lambda a, b: (l, 0, 0))
    in_specs = [tok(D, 0), pl.BlockSpec((gb, N_MOD, D), lambda a, b: (a, 0, 0)),
                tok(GLA_DV, 0), tok(GLA_DV, 2), tok(D, 3), tok(D, 4)]
    args = [x, mod, oa, proj, proj, proj]
    if combine:
        os_, ls_ = ob_parts
        head_major = pl.BlockSpec((None, GROUP_HEADS, tm, HEAD_DIM), lambda a, b: (a, 0, b, 0))
        in_specs += [head_major] * N_GROUPS + [tok(LANES, 0)] * N_GROUPS
        args += list(os_) + list(ls_)
    else:
        in_specs += [tok(B_OUT, 0)]
        args += list(ob_parts)
    in_specs += [const((GLA_DV, D)), const((B_OUT, D)), const((D, D))]
    args += [wpa, wpb, wo]
    return pl.pallas_call(
        functools.partial(_merge_kernel, combine=combine),
        grid=(G // gb, R // rb),
        in_specs=in_specs,
        out_specs=tok(D, 0),
        out_shape=jax.ShapeDtypeStruct((G, R, D), F32),
        compiler_params=_cparams("arbitrary", "arbitrary"),
        name="merge_combine" if combine else "merge",
    )(*args)


def _mlp_kernel(x_ref, mod_ref, g_ref, wu_ref, wd_ref, out_ref, u_scr, acc_scr):
    f = pl.program_id(2)
    gb, rb, d = x_ref.shape
    tm = gb * rb

    @pl.when(f == 0)
    def _():
        mod = mod_ref[...]
        u = _norm_mod(x_ref[...], g_ref[...], mod[:, 3:4, :], mod[:, 4:5, :])
        u_scr[...] = u.reshape(tm, d).astype(BF16)
        acc_scr[...] = jnp.zeros_like(acc_scr)

    hid = jnp.maximum(_dot(u_scr[...], wu_ref[...].astype(BF16)), 0.0)
    acc_scr[...] += _dot((hid * hid).astype(BF16), wd_ref[...].astype(BF16))

    @pl.when(f == pl.num_programs(2) - 1)
    def _():
        out_ref[...] = x_ref[...] + mod_ref[...][:, 5:6, :] * acc_scr[...].reshape(gb, rb, d)


def _mlp_call(x, mod, g, wu, wd, l, gb, rb, tf):
    G, R, D = x.shape
    F = wu.shape[2]
    tm = gb * rb
    return pl.pallas_call(
        _mlp_kernel,
        grid=(G // gb, R // rb, F // tf),
        in_specs=[pl.BlockSpec((gb, rb, D), lambda a, b, f: (a, b, 0)),
                  pl.BlockSpec((gb, N_MOD, D), lambda a, b, f: (a, 0, 0)),
                  pl.BlockSpec((1, D), lambda a, b, f: (0, 0)),
                  pl.BlockSpec((None, D, tf), lambda a, b, f: (l, 0, f)),
                  pl.BlockSpec((None, tf, D), lambda a, b, f: (l, f, 0))],
        out_specs=pl.BlockSpec((gb, rb, D), lambda a, b, f: (a, b, 0)),
        out_shape=jax.ShapeDtypeStruct((G, R, D), F32),
        scratch_shapes=[pltpu.VMEM((tm, D), BF16), pltpu.VMEM((tm, D), F32)],
        compiler_params=_cparams("arbitrary", "arbitrary", "arbitrary"),
        name="mlp",
    )(x, mod, g, wu, wd)


def _t5_bucket(dist):
    dist = np.asarray(dist)
    large = REL_MAX_EXACT + (np.log(np.maximum(dist, 1) / REL_MAX_EXACT)
                             / np.log(REL_MAX_DIST / REL_MAX_EXACT)
                             * (REL_BUCKETS - REL_MAX_EXACT)).astype(np.int32)
    large = np.minimum(large, REL_BUCKETS - 1)
    return np.where(dist < REL_MAX_EXACT, dist, large).astype(np.int32)


def _tap_bias(rel_bias, g):
    d = DIL_PAIRS[g][1]
    idx = _t5_bucket(d * np.arange(TAPS))
    return rel_bias[idx][:, g * GROUP_HEADS:(g + 1) * GROUP_HEADS].astype(F32)


def _prompt_bias(rel_bias, g):
    tb = _tap_bias(rel_bias, g)
    w = jnp.concatenate([tb[::-1].T, jnp.full((GROUP_HEADS, Q_BLOCK), NEG, F32)], axis=1)
    flat = jnp.tile(w, (1, Q_BLOCK))[:, :Q_BLOCK * 2 * Q_BLOCK]
    return flat.reshape(GROUP_HEADS, Q_BLOCK, 2 * Q_BLOCK)


def _dist_bias(tb, d, n):
    h = tb.shape[1]
    v = jnp.concatenate([tb[:, None, :], jnp.full((TAPS, d - 1, h), NEG, F32)], axis=1)
    v = v.reshape(TAPS * d, h)
    if n > TAPS * d:
        v = jnp.concatenate([v, jnp.full((n - TAPS * d, h), NEG, F32)], axis=0)
    return v[:n].T


def _sample_bias(rel_bias, ts, widths):
    H = GROUP_HEADS
    heads = np.arange(H)
    same = jnp.asarray(heads[:, None] == heads[None, :])
    cache_parts, new_parts = [], []
    for g, (w, d) in enumerate(DIL_PAIRS):
        w_eff = widths[g]
        bv = _dist_bias(_tap_bias(rel_bias, g), d, w_eff + ts)
        rows = jnp.stack([bv[:, t + 1:t + 1 + w_eff][:, ::-1] for t in range(ts)], axis=1)
        if g == N_GROUPS - 1:
            rows = rows.reshape(H, ts, w_eff // d, d)[..., :ts].reshape(H, ts, (w_eff // d) * ts)
        cache_parts.append(rows.reshape(H * ts, -1))
        tri = jnp.stack([jnp.concatenate([bv[:, :t + 1][:, ::-1],
                                          jnp.full((H, ts - 1 - t), NEG, F32)], axis=1)
                         for t in range(ts)], axis=1)
        blk = jnp.where(same[:, None, :, None], tri[:, :, None, :], NEG).reshape(H * ts, H * ts)
        new_parts.append(jnp.concatenate([blk, jnp.full((H * ts, LANES - H * ts), NEG, F32)], axis=1))
    return jnp.concatenate(cache_parts + new_parts, axis=1)


def _window_rows(qkv, g, lo, lead):
    k0 = (SEG_BK - SEG_BQ + g) * GROUP_HEADS
    v0 = (SEG_BV - SEG_BQ + g) * GROUP_HEADS
    kv = jnp.stack([qkv[:, k0:k0 + GROUP_HEADS, lo:], qkv[:, v0:v0 + GROUP_HEADS, lo:]], axis=1)
    kv = jnp.transpose(kv, (0, 3, 1, 2, 4))
    return kv.reshape(lead + kv.shape[2:])


def kernel(x_prompt, x_sample, c_prompt, c_sample, state_gla, cache_win1, cache_win2, cache_win3,
           rel_bias, norm1_g, norm2_g, w_mod, b_mod, w_in, w_alpha, b_alpha, gla_norm_g,
           qn_g, kn_g, w_pa, w_pb, w_o, w_up, w_down):
    B, T, D = x_prompt.shape
    NS, TS, _ = x_sample.shape
    L = w_mod.shape[0]
    widths = [c.shape[2] for c in (cache_win1, cache_win2, cache_win3)]
    d2 = DIL_PAIRS[2][1]
    assert D == D_MODEL and TS == 8 and T % SUPER == 0
    assert widths[2] % d2 == 0 and TS <= d2
    caches = (cache_win1.astype(F32).reshape(L, NS, widths[0] * CACHE_ROW, LANES),
              cache_win2.astype(F32).reshape(L, NS, widths[1] * CACHE_ROW, LANES),
              cache_win3.astype(F32).reshape(L, NS, widths[2] // d2, d2 * CACHE_ROW, LANES))
    state = state_gla.astype(F32)

    mods = _mod_call(jnp.concatenate([c_prompt, c_sample], axis=0).astype(F32), w_mod, b_mod)
    mods = mods.reshape(L, B + NS, N_MOD, D)

    assert w_in.shape[2] == W_IN_COLS
    w_t = jnp.swapaxes(w_in, 1, 2).astype(F32).reshape(L * W_IN_COLS, D)
    w_al = jnp.pad(w_alpha, ((0, 0), (0, LANES - GLA_RANK), (0, 0))).astype(BF16)
    w_pa_b, w_pb_b, w_o_b = w_pa.astype(BF16), w_pb.astype(BF16), w_o.astype(BF16)
    w_up32, w_down32 = w_up.astype(F32), w_down.astype(F32)

    bias_p = [_prompt_bias(rel_bias, g) for g in range(N_GROUPS)]
    bias_s = _sample_bias(rel_bias, TS, widths)

    tm = 1024
    xp = x_prompt.astype(F32)
    xs = x_sample.astype(F32)
    gla_p = []
    gla_s_all = None
    win_p = [[] for _ in range(N_GROUPS)]
    win_s = [[] for _ in range(N_GROUPS)]
    row2 = lambda v: v.reshape(1, -1)
    for l in range(L):
        mod_p, mod_s = mods[l, :B], mods[l, B:]
        pre = (row2(norm1_g[l]), w_t, w_al, l, row2(b_alpha[l]))
        qk_gain = (row2(qn_g[l]), row2(kn_g[l]))
        gn = row2(gla_norm_g[l])

        u, la = _prenorm_call(xp, mod_p, *pre, gb=1, rb=tm)
        proj = _inproj_main_call(u, w_t, l, tm=2 * tm)
        qkv = _inproj_qkv_call(u, w_t, l, *qk_gain, tm=tm)
        oa, s_fin = _gla_prompt_call(proj, la, gn, tb=512)
        outs = [_dil_prompt_call(qkv, bias_p[g], g, DIL_PAIRS[g][1]) for g in range(N_GROUPS)]
        xp = _merge_call(xp, mod_p, oa, proj, ([o for o, _ in outs], [s for _, s in outs]),
                         w_pa_b, w_pb_b, w_o_b, l, gb=1, rb=tm // 2)
        xp = _mlp_call(xp, mod_p, row2(norm2_g[l]), w_up32, w_down32, l, gb=1, rb=tm, tf=1024)
        gla_p.append(s_fin)
        for g, (w, d) in enumerate(DIL_PAIRS):
            keep = min(w, T)
            win_p[g].append(_window_rows(qkv, g, T - keep, (B, keep)))

        u, la = _prenorm_call(xs, mod_s, *pre, gb=NS, rb=TS)
        proj = _inproj_main_call(u, w_t, l, tm=NS * TS)
        qkv = _inproj_qkv_call(u, w_t, l, *qk_gain, tm=NS * TS)
        proj = proj.reshape(NS, TS, MAIN_W)
        oa, gla_s_all = _gla_sample_call(proj, la, state, l, gn, sb=8, acc=gla_s_all)
        ob = _dil_sample_call(qkv, caches, l, bias_s, NS, TS, sb=2)
        xs = _merge_call(xs, mod_s, oa, proj, (ob,), w_pa_b, w_pb_b, w_o_b, l, gb=NS, rb=TS)
        xs = _mlp_call(xs, mod_s, row2(norm2_g[l]), w_up32, w_down32, l, gb=NS, rb=TS, tf=1024)
        for g in range(N_GROUPS):
            win_s[g].append(_window_rows(qkv, g, 0, (NS, TS)))

    return (xp.astype(x_prompt.dtype), xs.astype(x_sample.dtype), jnp.stack(gla_p), gla_s_all,
            jnp.stack(win_p[0]), jnp.stack(win_s[0]), jnp.stack(win_p[1]), jnp.stack(win_s[1]),
            jnp.stack(win_p[2]), jnp.stack(win_s[2]))
```

```python
import functools

import numpy as np
import jax
import jax.numpy as jnp
from jax import lax
from jax.experimental import pallas as pl
from jax.experimental.pallas import tpu as pltpu

F32 = jnp.float32
BF16 = jnp.bfloat16

D_MODEL = 1024
GLA_HEADS = 4
GLA_DK_HEAD = 128
GLA_DV_HEAD = 256
GLA_DK = GLA_HEADS * GLA_DK_HEAD
GLA_DV = GLA_HEADS * GLA_DV_HEAD
GLA_RANK = 16
GLA_GATE_NORM = 16.0
GLA_CHUNK = 128
DIL_PAIRS = ((128, 1), (512, 4), (2048, 16))
N_GROUPS = 3
GROUP_HEADS = 4
HEAD_DIM = 128
B_WIDTH = N_GROUPS * GROUP_HEADS * HEAD_DIM
B_OUT = GROUP_HEADS * HEAD_DIM
TAPS = 129
Q_BLOCK = 128
REL_BUCKETS = 32
REL_MAX_EXACT = 16
REL_MAX_DIST = 2048
D_FF = 4 * D_MODEL
N_MOD = 6
EPS = 1e-6
NEG = -1e30

LANES = 128
CACHE_ROW = 2 * GROUP_HEADS

W_IN_SIZES = (GLA_DK, GLA_DK, GLA_DV, GLA_DV, GLA_RANK, B_WIDTH, B_WIDTH, B_WIDTH, D_MODEL, D_MODEL)
W_IN_OFFS = tuple(int(v) for v in np.concatenate([[0], np.cumsum(W_IN_SIZES)]))
W_IN_COLS = W_IN_OFFS[-1]
MAIN_SEG = 1024
MAIN_W = 2 * GLA_DK + 2 * GLA_DV + 2 * D_MODEL
SEG_BQ, SEG_BK, SEG_BV = 0, N_GROUPS, 2 * N_GROUPS
N_QKV = 3 * N_GROUPS * GROUP_HEADS
SUPER = 2048
LSE_LANES = LANES // GROUP_HEADS

VMEM_LIMIT = 48 * 1024 * 1024


def _cparams(*sem):
    return pltpu.CompilerParams(dimension_semantics=sem, vmem_limit_bytes=VMEM_LIMIT)


def _dot(a, b):
    return jnp.dot(a, b, preferred_element_type=F32)


def _dot_nt(a, b):
    return lax.dot_general(a, b, (((1,), (1,)), ((), ())), preferred_element_type=F32)


def _dot_tn(a, b):
    return lax.dot_general(a, b, (((0,), (0,)), ((), ())), preferred_element_type=F32)


def _sigmoid(x):
    return 1.0 / (1.0 + jnp.exp(-x))


def _norm_mod(x, g, shift, scale):
    ms = jnp.mean(x * x, axis=-1, keepdims=True)
    n = x * lax.rsqrt(ms + EPS) * g
    return n * (1.0 + scale) + shift


def _rms_lanes(x, g):
    ms = jnp.mean(x * x, axis=-1, keepdims=True)
    return x * lax.rsqrt(ms + EPS) * g


def _split2(x):
    hi = x.astype(BF16)
    lo = (x - hi.astype(F32)).astype(BF16)
    return hi, lo


def _mod_kernel(c_ref, w_ref, b_ref, o_ref):
    c = c_ref[...]
    a = (c * _sigmoid(c)).astype(BF16)
    o_ref[...] = _dot(a, w_ref[...].astype(BF16)) + b_ref[...]


def _mod_call(c_all, w_mod, b_mod):
    L, D, N = w_mod.shape
    nb = c_all.shape[0]
    tn = 1024
    return pl.pallas_call(
        _mod_kernel,
        grid=(L, N // tn),
        in_specs=[pl.BlockSpec((nb, D), lambda l, n: (0, 0)),
                  pl.BlockSpec((None, D, tn), lambda l, n: (l, 0, n)),
                  pl.BlockSpec((None, 1, tn), lambda l, n: (l, 0, n))],
        out_specs=pl.BlockSpec((None, nb, tn), lambda l, n: (l, 0, n)),
        out_shape=jax.ShapeDtypeStruct((L, nb, N), F32),
        compiler_params=_cparams("arbitrary", "arbitrary"),
        name="adaln_mod",
    )(c_all, w_mod, b_mod.reshape(L, 1, N))


def _prenorm_kernel(x_ref, mod_ref, g_ref, wglr_ref, wal_ref, bal_ref, u_ref, la_ref):
    gb, rb, d = x_ref.shape
    tm = gb * rb
    mod = mod_ref[...]
    u = _norm_mod(x_ref[...], g_ref[...], mod[:, 0:1, :], mod[:, 1:2, :])
    ub = u.reshape(tm, d).astype(BF16)
    u_ref[...] = ub
    glr = _dot_nt(ub, wglr_ref[...].astype(BF16))
    z = _dot(glr.astype(BF16), wal_ref[...]) + bal_ref[...]
    la = (jnp.minimum(z, 0.0) - jnp.log1p(jnp.exp(-jnp.abs(z)))) * (1.0 / GLA_GATE_NORM)
    la_ref[...] = la.reshape(gb, rb, GLA_DK)


def _prenorm_call(x, mod, g, w_t, wal, l, bal, gb, rb):
    G, R, D = x.shape
    tm = gb * rb
    na, nr = G // gb, R // rb
    const2 = lambda a, b: (0, 0)
    return pl.pallas_call(
        _prenorm_kernel,
        grid=(na, nr),
        in_specs=[pl.BlockSpec((gb, rb, D), lambda a, b: (a, b, 0)),
                  pl.BlockSpec((gb, N_MOD, D), lambda a, b: (a, 0, 0)),
                  pl.BlockSpec((1, D), const2),
                  pl.BlockSpec((pl.Element(LANES), pl.Element(D)),
                               lambda a, b: (pl.multiple_of(l * W_IN_COLS + W_IN_OFFS[4], 16), 0)),
                  pl.BlockSpec((None, LANES, GLA_DK), lambda a, b: (l, 0, 0)),
                  pl.BlockSpec((1, GLA_DK), const2)],
        out_specs=[pl.BlockSpec((None, tm, D), lambda a, b: (a, b, 0)),
                   pl.BlockSpec((gb, rb, GLA_DK), lambda a, b: (a, b, 0))],
        out_shape=[jax.ShapeDtypeStruct((na, nr * tm, D), BF16),
                   jax.ShapeDtypeStruct((G, R, GLA_DK), F32)],
        compiler_params=_cparams("arbitrary", "arbitrary"),
        name="prenorm",
    )(x, mod, g, w_t, wal, bal)


def _inproj_main_kernel(u_ref, w_ref, proj_ref):
    proj_ref[...] = _dot_nt(u_ref[...], w_ref[...].astype(BF16)).astype(proj_ref.dtype)


def _inproj_main_call(u, w_t, l, tm, out_dtype):
    A, N, D = u.shape
    n_gla = W_IN_OFFS[4] // MAIN_SEG

    def col(s):
        return MAIN_SEG * s + jnp.where(s < n_gla, 0, W_IN_OFFS[8] - W_IN_OFFS[4])

    return pl.pallas_call(
        _inproj_main_kernel,
        grid=(A, N // tm, MAIN_W // MAIN_SEG),
        in_specs=[pl.BlockSpec((None, tm, D), lambda a, b, s: (a, b, 0)),
                  pl.BlockSpec((pl.Element(MAIN_SEG), pl.Element(D)),
                               lambda a, b, s: (pl.multiple_of(l * W_IN_COLS + col(s), 16), 0))],
        out_specs=pl.BlockSpec((None, tm, MAIN_SEG), lambda a, b, s: (a, b, s)),
        out_shape=jax.ShapeDtypeStruct((A, N, MAIN_W), out_dtype),
        compiler_params=_cparams("arbitrary", "arbitrary", "arbitrary"),
        name="in_proj_main",
    )(u, w_t)


def _inproj_qkv_kernel(u_ref, w_ref, qn_ref, kn_ref, qkv_ref):
    s = pl.program_id(2)
    tm = u_ref.shape[0]
    rc = min(tm, 512)
    nh = B_WIDTH // HEAD_DIM

    def chunks(finish):
        w = w_ref[...].astype(BF16)
        for c in range(tm // rc):
            r = _dot_nt(u_ref[c * rc:(c + 1) * rc, :], w)
            for h in range(nh):
                qkv_ref[h, c * rc:(c + 1) * rc, :] = finish(r[:, h * HEAD_DIM:(h + 1) * HEAD_DIM])

    @pl.when(s == 0)
    def _():
        chunks(lambda r: _rms_lanes(r, qn_ref[...]))

    @pl.when(s == 1)
    def _():
        chunks(lambda r: _rms_lanes(r, kn_ref[...]))

    @pl.when(s == 2)
    def _():
        chunks(lambda r: r)


def _inproj_qkv_call(u, w_t, l, qn, kn, tm):
    A, N, D = u.shape
    nh = B_WIDTH // HEAD_DIM
    const2 = lambda a, b, s: (0, 0)
    return pl.pallas_call(
        _inproj_qkv_kernel,
        grid=(A, N // tm, N_QKV // nh),
        in_specs=[pl.BlockSpec((None, tm, D), lambda a, b, s: (a, b, 0)),
                  pl.BlockSpec((pl.Element(B_WIDTH), pl.Element(D)),
                               lambda a, b, s: (pl.multiple_of(
                                   l * W_IN_COLS + W_IN_OFFS[5] + B_WIDTH * s, 16), 0)),
                  pl.BlockSpec((1, HEAD_DIM), const2),
                  pl.BlockSpec((1, HEAD_DIM), const2)],
        out_specs=pl.BlockSpec((None, nh, tm, HEAD_DIM), lambda a, b, s: (a, s, b, 0)),
        out_shape=jax.ShapeDtypeStruct((A, N_QKV, N, HEAD_DIM), F32),
        compiler_params=_cparams("arbitrary", "arbitrary", "arbitrary"),
        name="in_proj_qkv",
    )(u, w_t, qn, kn)


def _gla_unit(q, k, v, la, S, tril, lmat, ones_c):
    dk = la.shape[1]
    la2 = jnp.concatenate(_split2(la), axis=1)
    b2 = _dot(lmat, la2)
    b = b2[:, :dk] + b2[:, dk:]
    half = b.shape[0] // 2
    b_mid = b[half - 1:half, :]
    qs = q * (GLA_DK_HEAD ** -0.5)
    qe = (qs * jnp.exp(b)).astype(BF16)
    qm = (qs * jnp.exp(b - b_mid)).astype(BF16)
    km = (k * jnp.exp(b_mid - b)).astype(BF16)
    vb = v.astype(BF16)
    sc = jnp.where(tril, _dot_nt(qm, km), 0.0)
    o = _dot(jnp.concatenate([qe, sc.astype(BF16)], axis=1),
             jnp.concatenate([S.astype(BF16), vb], axis=0))
    b_last = b[b.shape[0] - 1:, :]
    kd = (k * jnp.exp(b_last - b)).astype(BF16)
    d2 = _dot_tn(la2, ones_c)
    dec = jnp.exp(d2[:dk] + d2[dk:])
    s_new = S * jnp.concatenate([dec, dec], axis=1) + _dot_tn(kd, vb)
    return o, s_new


def _gla_prompt_kernel(q_ref, k_ref, v_ref, la_ref, gn_ref, o_ref, sfin_ref, s_scr):
    j = pl.program_id(0)
    nb, tb, _ = q_ref.shape
    C = GLA_CHUNK

    @pl.when(j == 0)
    def _():
        s_scr[...] = jnp.zeros_like(s_scr)

    row = lax.broadcasted_iota(jnp.int32, (C, C), 0)
    col = lax.broadcasted_iota(jnp.int32, (C, C), 1)
    tril = col <= row
    lmat = tril.astype(BF16)
    ones_c = jnp.ones((C, LANES), BF16)
    gn = gn_ref[...]
    for b in range(nb):
        for h in range(GLA_HEADS):
            ksl = slice(h * GLA_DK_HEAD, (h + 1) * GLA_DK_HEAD)
            vsl = slice(h * GLA_DV_HEAD, (h + 1) * GLA_DV_HEAD)
            S = s_scr[b, h]
            for c in range(tb // C):
                rows = slice(c * C, (c + 1) * C)
                o, S = _gla_unit(q_ref[b, rows, ksl].astype(F32), k_ref[b, rows, ksl].astype(F32),
                                 v_ref[b, rows, vsl].astype(F32),
                                 la_ref[b, rows, ksl], S, tril, lmat, ones_c)
                o_ref[b, rows, vsl] = _rms_lanes(o, gn)
            s_scr[b, h] = S

    @pl.when(j == pl.num_programs(0) - 1)
    def _():
        sfin_ref[...] = s_scr[...]


def _gla_prompt_call(proj, la, gn, tb):
    B, T, _ = proj.shape
    st = (B, GLA_HEADS, GLA_DK_HEAD, GLA_DV_HEAD)
    return pl.pallas_call(
        _gla_prompt_kernel,
        grid=(T // tb,),
        in_specs=[pl.BlockSpec((B, tb, GLA_DK), lambda j: (0, j, 0)),
                  pl.BlockSpec((B, tb, GLA_DK), lambda j: (0, j, 1)),
                  pl.BlockSpec((B, tb, GLA_DV), lambda j: (0, j, 1)),
                  pl.BlockSpec((B, tb, GLA_DK), lambda j: (0, j, 0)),
                  pl.BlockSpec((1, GLA_DV_HEAD), lambda j: (0, 0))],
        out_specs=[pl.BlockSpec((B, tb, GLA_DV), lambda j: (0, j, 0)),
                   pl.BlockSpec(st, lambda j: (0, 0, 0, 0))],
        out_shape=[jax.ShapeDtypeStruct((B, T, GLA_DV), F32), jax.ShapeDtypeStruct(st, F32)],
        scratch_shapes=[pltpu.VMEM(st, F32)],
        compiler_params=_cparams("arbitrary"),
        name="gla_prompt",
    )(proj, proj, proj, la, gn)


def _gla_sample_kernel(q_ref, k_ref, v_ref, la_ref, sin_ref, gn_ref, *rest):
    o_ref, sout_ref = rest[-2:]
    sb, ts, _ = q_ref.shape
    R = sb * ts
    row = lax.broadcasted_iota(jnp.int32, (R, R), 0)
    col = lax.broadcasted_iota(jnp.int32, (R, R), 1)
    shift = ts.bit_length() - 1
    same = jnp.right_shift(row, shift) == jnp.right_shift(col, shift)
    tril = same & (col <= row)
    lmat = tril.astype(BF16)
    bones = same.astype(BF16)
    ones_t = jnp.ones((ts, LANES), F32)
    gn = gn_ref[...]
    for h in range(GLA_HEADS):
        ksl = slice(h * GLA_DK_HEAD, (h + 1) * GLA_DK_HEAD)
        vsl = slice(h * GLA_DV_HEAD, (h + 1) * GLA_DV_HEAD)
        q = q_ref[:, :, ksl].reshape(R, GLA_DK_HEAD) * (GLA_DK_HEAD ** -0.5)
        k = k_ref[:, :, ksl].reshape(R, GLA_DK_HEAD)
        v = v_ref[:, :, vsl].reshape(R, GLA_DV_HEAD)
        la = la_ref[:, :, ksl].reshape(R, GLA_DK_HEAD)
        la_hi, la_lo = _split2(la)
        b = _dot(lmat, la_hi) + _dot(lmat, la_lo)
        btot = _dot(bones, la_hi) + _dot(bones, la_lo)
        qe = q * jnp.exp(b)
        ke = (k * jnp.exp(-b)).astype(BF16)
        kd = k * jnp.exp(btot - b)
        vb = v.astype(BF16)
        sc = jnp.where(tril, _dot_nt(qe.astype(BF16), ke), 0.0)
        o_intra = _dot(sc.astype(BF16), vb)
        la_hi32, la_lo32 = la_hi.astype(F32), la_lo.astype(F32)
        for i in range(sb):
            rs = slice(i * ts, (i + 1) * ts)
            S = sin_ref[i, h]
            o = _dot(qe[rs].astype(BF16), S.astype(BF16)) + o_intra[rs]
            kv = _dot_tn(kd[rs].astype(BF16), v[rs].astype(BF16))
            dec = jnp.exp(_dot_tn(la_hi32[rs].astype(BF16), ones_t.astype(BF16))
                          + _dot_tn(la_lo32[rs].astype(BF16), ones_t.astype(BF16)))
            sout_ref[i, h] = S * jnp.concatenate([dec, dec], axis=1) + kv
            o_ref[i, :, vsl] = _rms_lanes(o, gn)


def _gla_sample_call(proj, la, state, l, gn, sb, acc):
    NS, TS, _ = proj.shape
    st_shape = (None, sb, GLA_HEADS, GLA_DK_HEAD, GLA_DV_HEAD)
    st_spec = pl.BlockSpec(st_shape, lambda i: (l, i, 0, 0, 0))
    in_specs = [pl.BlockSpec((sb, TS, GLA_DK), lambda i: (i, 0, 0)),
                pl.BlockSpec((sb, TS, GLA_DK), lambda i: (i, 0, 1)),
                pl.BlockSpec((sb, TS, GLA_DV), lambda i: (i, 0, 1)),
                pl.BlockSpec((sb, TS, GLA_DK), lambda i: (i, 0, 0)),
                st_spec,
                pl.BlockSpec((1, GLA_DV_HEAD), lambda i: (0, 0))]
    args = [proj, proj, proj, la, state, gn]
    aliases = {}
    if acc is not None:
        in_specs.append(pl.BlockSpec(memory_space=pl.ANY))
        args.append(acc)
        aliases = {len(args) - 1: 1}
    return pl.pallas_call(
        _gla_sample_kernel,
        grid=(NS // sb,),
        in_specs=in_specs,
        out_specs=[pl.BlockSpec((sb, TS, GLA_DV), lambda i: (i, 0, 0)), st_spec],
        out_shape=[jax.ShapeDtypeStruct((NS, TS, GLA_DV), F32),
                   jax.ShapeDtypeStruct(state.shape, F32)],
        input_output_aliases=aliases,
        compiler_params=_cparams("arbitrary"),
        name="gla_sample",
    )(*args)


def _dil_prompt_kernel(q_ref, k_ref, v_ref, bias_ref, o_ref, lse_ref, kc_scr, vc_scr, *, d):
    n = pl.program_id(1)
    nsub = SUPER // (Q_BLOCK * d)
    scale = HEAD_DIM ** -0.5
    heads = range(GROUP_HEADS)

    @pl.when(n == 0)
    def _():
        kc_scr[...] = jnp.zeros_like(kc_scr)
        vc_scr[...] = jnp.zeros_like(vc_scr)

    col = lax.broadcasted_iota(jnp.int32, (Q_BLOCK, 2 * Q_BLOCK), 1)
    first_mask = jnp.where((col < Q_BLOCK) & (n == 0), NEG, 0.0)
    lane = lax.broadcasted_iota(jnp.int32, (Q_BLOCK, LANES), 1)
    ones = jnp.ones((Q_BLOCK, LANES), BF16)
    for p in range(d):
        cs = slice(p * Q_BLOCK, (p + 1) * Q_BLOCK)
        kp = [kc_scr[h, cs, :] for h in heads]
        vp = [vc_scr[h, cs, :] for h in heads]
        for i in range(nsub):
            start = p + d * Q_BLOCK * i
            rows = pl.ds(start, Q_BLOCK, stride=d) if d > 1 else pl.ds(start, Q_BLOCK)
            lses = []
            for h in heads:
                q = q_ref.at[h][rows, :].astype(BF16)
                kc = k_ref.at[h][rows, :].astype(BF16)
                vc = v_ref.at[h][rows, :].astype(BF16)
                bias = bias_ref[h] + first_mask if i == 0 else bias_ref[h]
                lg = _dot_nt(q, jnp.concatenate([kp[h], kc], axis=0)) * scale + bias
                m = jnp.max(lg, axis=-1, keepdims=True)
                pr = jnp.exp(lg - m).astype(BF16)
                v2 = jnp.concatenate([jnp.concatenate([vp[h], ones], axis=1),
                                      jnp.concatenate([vc, ones], axis=1)], axis=0)
                acc = _dot(pr, v2)
                s = acc[:, HEAD_DIM:]
                o_ref.at[h][rows, :] = acc[:, :HEAD_DIM] / s
                lses.append(m + jnp.log(s))
                kp[h], vp[h] = kc, vc
            packed = lses[-1]
            for h in reversed(heads[:-1]):
                packed = jnp.where(lane < (h + 1) * LSE_LANES, lses[h], packed)
            lse_ref[rows, :] = packed
        for h in heads:
            kc_scr[h, cs, :] = kp[h]
            vc_scr[h, cs, :] = vp[h]


def _dil_prompt_call(qkv, bias_g, g, d):
    B, _, T, _ = qkv.shape
    heads_blk = (None, GROUP_HEADS, SUPER, HEAD_DIM)

    def spec(seg):
        blk = seg - SEG_BQ + g
        return pl.BlockSpec(heads_blk, lambda b, n: (b, blk, n, 0))

    carry = pltpu.VMEM((GROUP_HEADS, d * Q_BLOCK, HEAD_DIM), BF16)
    return pl.pallas_call(
        functools.partial(_dil_prompt_kernel, d=d),
        grid=(B, T // SUPER),
        in_specs=[spec(SEG_BQ), spec(SEG_BK), spec(SEG_BV),
                  pl.BlockSpec((GROUP_HEADS, Q_BLOCK, 2 * Q_BLOCK), lambda b, n: (0, 0, 0))],
        out_specs=[pl.BlockSpec(heads_blk, lambda b, n: (b, 0, n, 0)),
                   pl.BlockSpec((None, SUPER, LANES), lambda b, n: (b, n, 0))],
        out_shape=[jax.ShapeDtypeStruct((B, GROUP_HEADS, T, HEAD_DIM), F32),
                   jax.ShapeDtypeStruct((B, T, LANES), F32)],
        scratch_shapes=[carry, carry],
        compiler_params=_cparams("arbitrary", "arbitrary"),
        name="dilated_prompt_g%d" % g,
    )(qkv, qkv, qkv, bias_g)


def _dil_sample_kernel(qkv_ref, c0_ref, c1_ref, c2_ref, bias_ref, o_ref):
    sb, ts, _ = o_ref.shape
    scale = HEAD_DIM ** -0.5
    heads = range(GROUP_HEADS)
    qrows = GROUP_HEADS * ts
    widths = (c0_ref.shape[1] // CACHE_ROW, c1_ref.shape[1] // CACHE_ROW, c2_ref.shape[1] * ts)
    zpad = jnp.zeros((LANES - qrows, HEAD_DIM), F32)
    for j in range(sb):
        def cache_rows(g, r):
            if g == 0:
                return c0_ref.at[j][pl.ds(r, widths[0], stride=CACHE_ROW), :].astype(BF16)
            if g == 1:
                return c1_ref.at[j][pl.ds(r, widths[1], stride=CACHE_ROW), :].astype(BF16)
            rows = c2_ref.at[j][:, pl.ds(r, ts, stride=CACHE_ROW), :]
            return rows.reshape(widths[2], HEAD_DIM).astype(BF16)

        def plane(seg, h):
            return qkv_ref[seg * GROUP_HEADS + h, j * ts:(j + 1) * ts, :]

        def planes(seg):
            return jnp.concatenate([plane(seg, h) for h in heads], axis=0)

        parts = []
        off = 0
        for g in range(N_GROUPS):
            lg = jnp.concatenate([_dot_nt(plane(g, h).astype(BF16), cache_rows(g, h)) for h in heads],
                                 axis=0)
            parts.append((lg * scale + bias_ref[:, off:off + widths[g]],
                          [cache_rows(g, GROUP_HEADS + h) for h in heads]))
            off += widths[g]
        for g in range(N_GROUPS):
            kn = jnp.concatenate([planes(N_GROUPS + g), zpad], axis=0).astype(BF16)
            vn = jnp.concatenate([planes(2 * N_GROUPS + g), zpad], axis=0).astype(BF16)
            parts.append((_dot_nt(planes(g).astype(BF16), kn) * scale + bias_ref[:, off:off + LANES], vn))
            off += LANES
        m = parts[0][0].max(axis=-1, keepdims=True)
        for lg, _ in parts[1:]:
            m = jnp.maximum(m, lg.max(axis=-1, keepdims=True))
        s = jnp.zeros((qrows, 1), F32)
        acc = jnp.zeros((qrows, HEAD_DIM), F32)
        for lg, vv in parts:
            p = jnp.exp(lg - m)
            s = s + jnp.sum(p, axis=-1, keepdims=True)
            if isinstance(vv, list):
                acc = acc + jnp.concatenate(
                    [_dot(p[h * ts:(h + 1) * ts].astype(BF16), vv[h]) for h in heads], axis=0)
            else:
                acc = acc + _dot(p.astype(BF16), vv)
        res = acc / s
        for h in range(GROUP_HEADS):
            o_ref[j, :, h * HEAD_DIM:(h + 1) * HEAD_DIM] = res[h * ts:(h + 1) * ts]


def _dil_sample_specs(caches, l, bias_s, ts, sb):
    c0, c1, c2 = caches
    in_specs = [pl.BlockSpec((None, N_QKV, sb * ts, HEAD_DIM), lambda i: (0, 0, i, 0)),
                pl.BlockSpec((None, sb) + c0.shape[2:], lambda i: (l, i, 0, 0)),
                pl.BlockSpec((None, sb) + c1.shape[2:], lambda i: (l, i, 0, 0)),
                pl.BlockSpec((None, sb, c2.shape[2], ts * CACHE_ROW, LANES),
                             lambda i: (l, i, 0, 0, 0)),
                pl.BlockSpec(bias_s.shape, lambda i: (0, 0))]
    return in_specs, pl.BlockSpec((sb, ts, B_OUT), lambda i: (i, 0, 0))


def _dil_sample_call(qkv, caches, l, bias_s, ns, ts, sb):
    in_specs, out_spec = _dil_sample_specs(caches, l, bias_s, ts, sb)
    return pl.pallas_call(
        _dil_sample_kernel,
        grid=(ns // sb,),
        in_specs=in_specs,
        out_specs=out_spec,
        out_shape=jax.ShapeDtypeStruct((ns, ts, B_OUT), F32),
        compiler_params=_cparams("arbitrary"),
        name="dilated_sample",
    )(qkv, *caches, bias_s)


def _merge_kernel(*refs, combine):
    if combine:
        (x_ref, mod_ref, oa_ref, gr_ref, ga_ref, gb_ref, o0_ref, o1_ref, o2_ref,
         l0_ref, l1_ref, l2_ref, wpa_ref, wpb_ref, wo_ref, out_ref) = refs
    else:
        (x_ref, mod_ref, oa_ref, gr_ref, ga_ref, gb_ref, ob_ref,
         wpa_ref, wpb_ref, wo_ref, out_ref) = refs
    gb, rb, d = x_ref.shape
    tm = gb * rb
    if combine:
        o_refs = (o0_ref, o1_ref, o2_ref)
        l_refs = (l0_ref, l1_ref, l2_ref)
        ls = [l[...].reshape(tm, LANES) for l in l_refs]
        m = jnp.maximum(jnp.maximum(ls[0], ls[1]), ls[2])
        es = [jnp.exp(l - m) for l in ls]
        den = es[0] + es[1] + es[2]
        ws = [e / den for e in es]
        parts = []
        for h in range(GROUP_HEADS):
            c = h * LSE_LANES
            parts.append(sum(ws[g][:, c:c + 1] * o_refs[g][h] for g in range(N_GROUPS)))
        ob = jnp.concatenate(parts, axis=1)
    else:
        ob = ob_ref[...].reshape(tm, B_OUT)
    gr = gr_ref[...].reshape(tm, GLA_DV).astype(F32)
    oa = oa_ref[...].reshape(tm, GLA_DV) * (gr * _sigmoid(gr))
    m1 = _dot(oa.astype(BF16), wpa_ref[...])
    m2 = _dot(ob.astype(BF16), wpb_ref[...])
    merged = (_sigmoid(ga_ref[...].reshape(tm, d).astype(F32)) * m1
              + _sigmoid(gb_ref[...].reshape(tm, d).astype(F32)) * m2)
    y = _dot(merged.astype(BF16), wo_ref[...]).reshape(gb, rb, d)
    out_ref[...] = x_ref[...] + mod_ref[...][:, 2:3, :] * y


def _merge_call(x, mod, oa, proj, ob_parts, wpa, wpb, wo, l, gb, rb):
    G, R, D = x.shape
    combine = len(ob_parts) > 1
    tm = gb * rb
    tok = lambda w, c: pl.BlockSpec((gb, rb, w), lambda a, b: (a, b, c))
    const = lambda shp: pl.BlockSpec((None,) + shp, lambda a, b: (l, 0, 0))
    in_specs = [tok(D, 0), pl.BlockSpec((gb, N_MOD, D), lambda a, b: (a, 0, 0)),
                tok(GLA_DV, 0), tok(GLA_DV, 2), tok(D, 3), tok(D, 4)]
    args = [x, mod, oa, proj, proj, proj]
    if combine:
        os_, ls_ = ob_parts
        head_major = pl.BlockSpec((None, GROUP_HEADS, tm, HEAD_DIM), lambda a, b: (a, 0, b, 0))
        in_specs += [head_major] * N_GROUPS + [tok(LANES, 0)] * N_GROUPS
        args += list(os_) + list(ls_)
    else:
        in_specs += [tok(B_OUT, 0)]
        args += list(ob_parts)
    in_specs += [const((GLA_DV, D)), const((B_OUT, D)), const((D, D))]
    args += [wpa, wpb, wo]
    return pl.pallas_call(
        functools.partial(_merge_kernel, combine=combine),
        grid=(G // gb, R // rb),
        in_specs=in_specs,
        out_specs=tok(D, 0),
        out_shape=jax.ShapeDtypeStruct((G, R, D), F32),
        compiler_params=_cparams("arbitrary", "arbitrary"),
        name="merge_combine" if combine else "merge",
    )(*args)


def _mlp_kernel(x_ref, mod_ref, g_ref, wu_ref, wd_ref, out_ref, u_scr, acc_scr):
    f = pl.program_id(2)
    gb, rb, d = x_ref.shape
    tm = gb * rb

    @pl.when(f == 0)
    def _():
        mod = mod_ref[...]
        u = _norm_mod(x_ref[...], g_ref[...], mod[:, 3:4, :], mod[:, 4:5, :])
        u_scr[...] = u.reshape(tm, d).astype(BF16)
        acc_scr[...] = jnp.zeros_like(acc_scr)

    hid = jnp.maximum(_dot(u_scr[...], wu_ref[...].astype(BF16)), 0.0)
    acc_scr[...] += _dot((hid * hid).astype(BF16), wd_ref[...].astype(BF16))

    @pl.when(f == pl.num_programs(2) - 1)
    def _():
        out_ref[...] = x_ref[...] + mod_ref[...][:, 5:6, :] * acc_scr[...].reshape(gb, rb, d)


def _mlp_call(x, mod, g, wu, wd, l, gb, rb, tf):
    G, R, D = x.shape
    F = wu.shape[2]
    tm = gb * rb
    return pl.pallas_call(
        _mlp_kernel,
        grid=(G // gb, R // rb, F // tf),
        in_specs=[pl.BlockSpec((gb, rb, D), lambda a, b, f: (a, b, 0)),
                  pl.BlockSpec((gb, N_MOD, D), lambda a, b, f: (a, 0, 0)),
                  pl.BlockSpec((1, D), lambda a, b, f: (0, 0)),
                  pl.BlockSpec((None, D, tf), lambda a, b, f: (l, 0, f)),
                  pl.BlockSpec((None, tf, D), lambda a, b, f: (l, f, 0))],
        out_specs=pl.BlockSpec((gb, rb, D), lambda a, b, f: (a, b, 0)),
        out_shape=jax.ShapeDtypeStruct((G, R, D), F32),
        scratch_shapes=[pltpu.VMEM((tm, D), BF16), pltpu.VMEM((tm, D), F32)],
        compiler_params=_cparams("arbitrary", "arbitrary", "arbitrary"),
        name="mlp",
    )(x, mod, g, wu, wd)


def _t5_bucket(dist):
    dist = np.asarray(dist)
    large = REL_MAX_EXACT + (np.log(np.maximum(dist, 1) / REL_MAX_EXACT)
                             / np.log(REL_MAX_DIST / REL_MAX_EXACT)
                             * (REL_BUCKETS - REL_MAX_EXACT)).astype(np.int32)
    large = np.minimum(large, REL_BUCKETS - 1)
    return np.where(dist < REL_MAX_EXACT, dist, large).astype(np.int32)


def _tap_bias(rel_bias, g):
    d = DIL_PAIRS[g][1]
    idx = _t5_bucket(d * np.arange(TAPS))
    return rel_bias[idx][:, g * GROUP_HEADS:(g + 1) * GROUP_HEADS].astype(F32)


def _prompt_bias(rel_bias, g):
    tb = _tap_bias(rel_bias, g)
    w = jnp.concatenate([tb[::-1].T, jnp.full((GROUP_HEADS, Q_BLOCK), NEG, F32)], axis=1)
    flat = jnp.tile(w, (1, Q_BLOCK))[:, :Q_BLOCK * 2 * Q_BLOCK]
    return flat.reshape(GROUP_HEADS, Q_BLOCK, 2 * Q_BLOCK)


def _dist_bias(tb, d, n):
    h = tb.shape[1]
    v = jnp.concatenate([tb[:, None, :], jnp.full((TAPS, d - 1, h), NEG, F32)], axis=1)
    v = v.reshape(TAPS * d, h)
    if n > TAPS * d:
        v = jnp.concatenate([v, jnp.full((n - TAPS * d, h), NEG, F32)], axis=0)
    return v[:n].T


def _sample_bias(rel_bias, ts, widths):
    H = GROUP_HEADS
    heads = np.arange(H)
    same = jnp.asarray(heads[:, None] == heads[None, :])
    cache_parts, new_parts = [], []
    for g, (w, d) in enumerate(DIL_PAIRS):
        w_eff = widths[g]
        bv = _dist_bias(_tap_bias(rel_bias, g), d, w_eff + ts)
        rows = jnp.stack([bv[:, t + 1:t + 1 + w_eff][:, ::-1] for t in range(ts)], axis=1)
        if g == N_GROUPS - 1:
            rows = rows.reshape(H, ts, w_eff // d, d)[..., :ts].reshape(H, ts, (w_eff // d) * ts)
        cache_parts.append(rows.reshape(H * ts, -1))
        tri = jnp.stack([jnp.concatenate([bv[:, :t + 1][:, ::-1],
                                          jnp.full((H, ts - 1 - t), NEG, F32)], axis=1)
                         for t in range(ts)], axis=1)
        blk = jnp.where(same[:, None, :, None], tri[:, :, None, :], NEG).reshape(H * ts, H * ts)
        new_parts.append(jnp.concatenate([blk, jnp.full((H * ts, LANES - H * ts), NEG, F32)], axis=1))
    return jnp.concatenate(cache_parts + new_parts, axis=1)


def _window_rows(qkv, g, lo, lead):
    k0 = (SEG_BK - SEG_BQ + g) * GROUP_HEADS
    v0 = (SEG_BV - SEG_BQ + g) * GROUP_HEADS
    kv = jnp.stack([qkv[:, k0:k0 + GROUP_HEADS, lo:], qkv[:, v0:v0 + GROUP_HEADS, lo:]], axis=1)
    kv = jnp.transpose(kv, (0, 3, 1, 2, 4))
    return kv.reshape(lead + kv.shape[2:])


def kernel(x_prompt, x_sample, c_prompt, c_sample, state_gla, cache_win1, cache_win2, cache_win3,
           rel_bias, norm1_g, norm2_g, w_mod, b_mod, w_in, w_alpha, b_alpha, gla_norm_g,
           qn_g, kn_g, w_pa, w_pb, w_o, w_up, w_down):
    B, T, D = x_prompt.shape
    NS, TS, _ = x_sample.shape
    L = w_mod.shape[0]
    widths = [c.shape[2] for c in (cache_win1, cache_win2, cache_win3)]
    d2 = DIL_PAIRS[2][1]
    assert D == D_MODEL and TS == 8 and T % SUPER == 0
    assert widths[2] % d2 == 0 and TS <= d2
    caches = (cache_win1.astype(F32).reshape(L, NS, widths[0] * CACHE_ROW, LANES),
              cache_win2.astype(F32).reshape(L, NS, widths[1] * CACHE_ROW, LANES),
              cache_win3.astype(F32).reshape(L, NS, widths[2] // d2, d2 * CACHE_ROW, LANES))
    state = state_gla.astype(F32)

    mods = _mod_call(jnp.concatenate([c_prompt, c_sample], axis=0).astype(F32), w_mod, b_mod)
    mods = mods.reshape(L, B + NS, N_MOD, D)

    assert w_in.shape[2] == W_IN_COLS
    w_t = jnp.swapaxes(w_in, 1, 2).astype(F32).reshape(L * W_IN_COLS, D)
    w_al = jnp.pad(w_alpha, ((0, 0), (0, LANES - GLA_RANK), (0, 0))).astype(BF16)
    w_pa_b, w_pb_b, w_o_b = w_pa.astype(BF16), w_pb.astype(BF16), w_o.astype(BF16)
    w_up32, w_down32 = w_up.astype(F32), w_down.astype(F32)

    bias_p = [_prompt_bias(rel_bias, g) for g in range(N_GROUPS)]
    bias_s = _sample_bias(rel_bias, TS, widths)

    tm = 1024
    xp = x_prompt.astype(F32)
    xs = x_sample.astype(F32)
    gla_p = []
    gla_s_all = None
    win_p = [[] for _ in range(N_GROUPS)]
    win_s = [[] for _ in range(N_GROUPS)]
    row2 = lambda v: v.reshape(1, -1)
    for l in range(L):
        mod_p, mod_s = mods[l, :B], mods[l, B:]
        pre = (row2(norm1_g[l]), w_t, w_al, l, row2(b_alpha[l]))
        qk_gain = (row2(qn_g[l]), row2(kn_g[l]))
        gn = row2(gla_norm_g[l])

        u, la = _prenorm_call(xp, mod_p, *pre, gb=1, rb=tm)
        proj = _inproj_main_call(u, w_t, l, tm=2 * tm, out_dtype=BF16)
        qkv = _inproj_qkv_call(u, w_t, l, *qk_gain, tm=tm)
        oa, s_fin = _gla_prompt_call(proj, la, gn, tb=512)
        outs = [_dil_prompt_call(qkv, bias_p[g], g, DIL_PAIRS[g][1]) for g in range(N_GROUPS)]
        xp = _merge_call(xp, mod_p, oa, proj, ([o for o, _ in outs], [s for _, s in outs]),
                         w_pa_b, w_pb_b, w_o_b, l, gb=1, rb=tm // 2)
        xp = _mlp_call(xp, mod_p, row2(norm2_g[l]), w_up32, w_down32, l, gb=1, rb=tm, tf=1024)
        gla_p.append(s_fin)
        for g, (w, d) in enumerate(DIL_PAIRS):
            keep = min(w, T)
            win_p[g].append(_window_rows(qkv, g, T - keep, (B, keep)))

        u, la = _prenorm_call(xs, mod_s, *pre, gb=NS, rb=TS)
        proj = _inproj_main_call(u, w_t, l, tm=NS * TS, out_dtype=F32)
        qkv = _inproj_qkv_call(u, w_t, l, *qk_gain, tm=NS * TS)
        proj = proj.reshape(NS, TS, MAIN_W)
        oa, gla_s_all = _gla_sample_call(proj, la, state, l, gn, sb=8, acc=gla_s_all)
        ob = _dil_sample_call(qkv, caches, l, bias_s, NS, TS, sb=2)
        xs = _merge_call(xs, mod_s, oa, proj, (ob,), w_pa_b, w_pb_b, w_o_b, l, gb=NS, rb=TS)
        xs = _mlp_call(xs, mod_s, row2(norm2_g[l]), w_up32, w_down32, l, gb=NS, rb=TS, tf=1024)
        for g in range(N_GROUPS):
            win_s[g].append(_window_rows(qkv, g, 0, (NS, TS)))

    return (xp.astype(x_prompt.dtype), xs.astype(x_sample.dtype), jnp.stack(gla_p), gla_s_all,
            jnp.stack(win_p[0]), jnp.stack(win_s[0]), jnp.stack(win_p[1]), jnp.stack(win_s[1]),
            jnp.stack(win_p[2]), jnp.stack(win_s[2]))
```
